```python
import math
import jax, jax.numpy as jnp
from jax import lax
import numpy as np

D_MODEL = 1024
BATCH = 4
SEQ = 8192
DEPTH = 2

HEAD_DIM = 64
N_MIXERS = 4
GROUP_WIDTH = D_MODEL // N_MIXERS
N_GROUP_HEADS = GROUP_WIDTH // HEAD_DIM
DIFF_HALF = HEAD_DIM // 2
ROPE_THETA = 10000.0
MAX_POS_OFFSET = 4096
Q_BLOCK = 128
MOBA_BLOCK = 256
MOBA_TOPK = 3
MOBA_Q_BLOCK = 32
NSA_CMP_LEN = 32
NSA_CMP_STRIDE = 16
NSA_CMP_HIDDEN = 256
NSA_SEL_BLOCK = 64
NSA_SEL_TOPK = 16
NSA_WINDOW = 512
NSA_N_BRANCH = 3
D_FF = 3584
N_EXPERTS = 8
TOP_K = 2
MOE_ROW_BLOCK = 256
PLE_DIM = 256
LN_EPS = 1e-5
RMS_EPS = 1e-5
DEEPNORM_ALPHA = (2 * DEPTH) ** 0.25
DEEPNORM_BETA = (8 * DEPTH) ** -0.25
IN_SPLITS = (GROUP_WIDTH,) * 10 + (HEAD_DIM,) * 6 + (N_GROUP_HEADS * NSA_N_BRANCH,)
N_IN = sum(IN_SPLITS)

kernel_name = 'hymba_style_sb_diff_moba_nsa_deepnorm_moe'


def layer_norm(x, g, b):
    xf = x.astype(jnp.float32)
    mu = jnp.mean(xf, -1, keepdims=True)
    var = jnp.mean(jnp.square(xf - mu), -1, keepdims=True)
    return ((xf - mu) * lax.rsqrt(var + LN_EPS) * g + b).astype(x.dtype)


def rope(x, positions):
    half = x.shape[-1] // 2
    inv_freq = ROPE_THETA ** (-jnp.arange(half, dtype=jnp.float32) / half)
    ang = positions.astype(jnp.float32)[:, :, None, None] * inv_freq
    cos, sin = jnp.cos(ang), jnp.sin(ang)
    x1 = x[..., :half].astype(jnp.float32)
    x2 = x[..., half:].astype(jnp.float32)
    return jnp.concatenate([x1 * cos - x2 * sin, x2 * cos + x1 * sin], -1).astype(x.dtype)


def split_heads(t, n_heads):
    B, S, _ = t.shape
    return t.reshape(B, S, n_heads, -1).transpose(0, 2, 1, 3)


def merge_heads(t):
    B, H, S, d = t.shape
    return t.transpose(0, 2, 1, 3).reshape(B, S, H * d)


def to_qblocks(t, blk):
    S, d = t.shape[-2], t.shape[-1]
    t = t.reshape(t.shape[:-2] + (S // blk, blk, d))
    return jnp.moveaxis(t, -3, 0)


def from_qblocks(o):
    o = jnp.moveaxis(o, 0, -3)
    return o.reshape(o.shape[:-3] + (o.shape[-3] * o.shape[-2], o.shape[-1]))


def masked_softmax(s, mask):
    s = jnp.where(mask, s, -jnp.inf)
    m = jnp.max(s, axis=-1, keepdims=True)
    m = jnp.where(jnp.isfinite(m), m, 0.0)
    e = jnp.exp(s - m)
    den = jnp.sum(e, axis=-1, keepdims=True)
    return e / jnp.where(den > 0, den, 1.0)


def stick_breaking_attention(q, k, v):
    B, H, S, d = q.shape
    scale = d ** -0.5
    kpos = jnp.arange(S)

    def block(args):
        qb, i = args
        qpos = i * Q_BLOCK + jnp.arange(Q_BLOCK)
        z = jnp.einsum('bhqd,bhkd->bhqk', qb, k).astype(jnp.float32) * scale
        past = kpos[None, :] < qpos[:, None]
        log_1m = jnp.where(past, jax.nn.log_sigmoid(-z), 0.0)
        later = lax.cumsum(log_1m, axis=3, reverse=True) - log_1m
        w = jnp.where(past, jnp.exp(jax.nn.log_sigmoid(z) + later), 0.0)
        return jnp.einsum('bhqk,bhkd->bhqd', w.astype(v.dtype), v)

    return from_qblocks(lax.map(block, (to_qblocks(q, Q_BLOCK), jnp.arange(S // Q_BLOCK))))


def diff_attention(q, k, v, lam, gain, lam_init):
    S, dh = q.shape[-2], q.shape[-1]
    scale = dh ** -0.5
    kpos = jnp.arange(S)

    def block(args):
        qb, i = args
        qpos = i * Q_BLOCK + jnp.arange(Q_BLOCK)
        s = jnp.einsum('bhcqd,bhckd->bhcqk', qb, k).astype(jnp.float32) * scale
        p = jax.nn.softmax(jnp.where(kpos[None, :] <= qpos[:, None], s, -jnp.inf), axis=-1)
        a = p[:, :, 0] - lam * p[:, :, 1]
        return jnp.einsum('bhqk,bhkd->bhqd', a.astype(v.dtype), v)

    o = from_qblocks(lax.map(block, (to_qblocks(q, Q_BLOCK), jnp.arange(S // Q_BLOCK))))
    of = o.astype(jnp.float32)
    of = of * lax.rsqrt(jnp.mean(of * of, -1, keepdims=True) + RMS_EPS) * gain
    return (of * (1.0 - lam_init)).astype(v.dtype)


def moba_attention(q, k, v):
    B, H, S, d = q.shape
    scale = d ** -0.5
    nb = -(-S // MOBA_BLOCK)
    pad = nb * MOBA_BLOCK - S
    kp = jnp.pad(k, ((0, 0), (0, 0), (0, pad), (0, 0)))
    vp = jnp.pad(v, ((0, 0), (0, 0), (0, pad), (0, 0)))
    kb = kp.reshape(B, H, nb, MOBA_BLOCK, d)
    vb = vp.reshape(B, H, nb, MOBA_BLOCK, d)
    k_mean = jnp.mean(kb.astype(jnp.float32), axis=3).astype(k.dtype)
    n_top = min(MOBA_TOPK, nb)
    n_g = n_top * MOBA_BLOCK
    blk_ids = jnp.arange(nb)
    in_blk = jnp.arange(MOBA_BLOCK)
    bi = jnp.arange(B)[:, None, None, None]
    hi = jnp.arange(H)[None, :, None, None]

    def block(args):
        qb, i = args
        q0 = i * MOBA_Q_BLOCK
        qpos = q0 + jnp.arange(MOBA_Q_BLOCK)
        own = q0 // MOBA_BLOCK
        gate = jnp.einsum('bhqd,bhnd->bhqn', qb, k_mean).astype(jnp.float32)
        gate = jnp.where(blk_ids < own, gate, -jnp.inf)
        _, sel = lax.top_k(gate, n_top)
        k_g = kb[bi, hi, sel]
        v_g = vb[bi, hi, sel]
        s_g = jnp.einsum('bhqd,bhqnld->bhqnl', qb, k_g).astype(jnp.float32) * scale
        m_g = jnp.broadcast_to((sel < own)[..., None], s_g.shape)
        k_o = lax.dynamic_slice_in_dim(kp, own * MOBA_BLOCK, MOBA_BLOCK, axis=2)
        v_o = lax.dynamic_slice_in_dim(vp, own * MOBA_BLOCK, MOBA_BLOCK, axis=2)
        s_o = jnp.einsum('bhqd,bhld->bhql', qb, k_o).astype(jnp.float32) * scale
        m_o = jnp.broadcast_to((own * MOBA_BLOCK + in_blk)[None, :] <= qpos[:, None], s_o.shape)
        p = masked_softmax(jnp.concatenate([s_g.reshape(B, H, MOBA_Q_BLOCK, n_g), s_o], -1),
                           jnp.concatenate([m_g.reshape(B, H, MOBA_Q_BLOCK, n_g), m_o], -1)).astype(v.dtype)
        return (jnp.einsum('bhqnl,bhqnld->bhqd', p[..., :n_g].reshape(s_g.shape), v_g)
                + jnp.einsum('bhql,bhld->bhqd', p[..., n_g:], v_o))

    return from_qblocks(lax.map(block, (to_qblocks(q, MOBA_Q_BLOCK), jnp.arange(S // MOBA_Q_BLOCK))))


def compress_blocks(t, pos_enc, w1, w2):
    B, S, d = t.shape
    n_span = NSA_CMP_LEN // NSA_CMP_STRIDE
    c = t.reshape(B, S // NSA_CMP_STRIDE, NSA_CMP_STRIDE, d)
    n_chunks = c.shape[1]
    blocks = jnp.concatenate([c[:, j:n_chunks - n_span + 1 + j] for j in range(n_span)], axis=2)
    blocks = (blocks + pos_enc).reshape(B, n_chunks - n_span + 1, NSA_CMP_LEN * d)
    return jax.nn.silu(blocks @ w1) @ w2


def nsa_attention(q, qr, kc_tok, vc_tok, ks, vs, kw, vw, gates, cmp_pos, cmp_w1, cmp_w2):
    B, S, H, d = q.shape
    scale = d ** -0.5
    kc = compress_blocks(kc_tok, cmp_pos[0], cmp_w1[0], cmp_w2[0])
    vc = compress_blocks(vc_tok, cmp_pos[1], cmp_w1[1], cmp_w2[1])
    n_cmp = kc.shape[1]
    cmp_start = jnp.arange(n_cmp) * NSA_CMP_STRIDE
    cmp_end = cmp_start + NSA_CMP_LEN - 1
    n_sel = S // NSA_SEL_BLOCK
    k_top = min(NSA_SEL_TOPK, n_sel)
    sel_start = jnp.arange(n_sel) * NSA_SEL_BLOCK
    overlap = ((cmp_start[:, None] < sel_start[None, :] + NSA_SEL_BLOCK)
               & (cmp_start[:, None] + NSA_CMP_LEN > sel_start[None, :])).astype(jnp.float32)
    ksb = ks.reshape(B, n_sel, NSA_SEL_BLOCK, d)
    vsb = vs.reshape(B, n_sel, NSA_SEL_BLOCK, d)
    kw_pad = jnp.pad(kw, ((0, 0), (NSA_WINDOW, 0), (0, 0)))
    vw_pad = jnp.pad(vw, ((0, 0), (NSA_WINDOW, 0), (0, 0)))
    bi = jnp.arange(B)[:, None, None]
    sel_ids = jnp.arange(n_sel)
    in_blk = jnp.arange(NSA_SEL_BLOCK)
    win_off = jnp.arange(NSA_WINDOW + Q_BLOCK)

    def block(args):
        qb, qrb, gb, i = args
        q0 = i * Q_BLOCK
        qpos = q0 + jnp.arange(Q_BLOCK)
        s_c = jnp.einsum('bhqd,bnd->bhqn', qb, kc).astype(jnp.float32) * scale
        p_c = masked_softmax(s_c, cmp_end[None, :] <= qpos[:, None])
        o_c = jnp.einsum('bhqn,bnd->bhqd', p_c.astype(vc.dtype), vc)
        imp = jnp.einsum('bhqn,nm->bqm', p_c, overlap)
        qblk = qpos // NSA_SEL_BLOCK
        causal_blk = sel_ids[None, :] <= qblk[:, None]
        forced = (sel_ids[None, :] == 0) | (sel_ids[None, :] == qblk[:, None]) | (sel_ids[None, :] == qblk[:, None] - 1)
        imp = jnp.where(forced, jnp.inf, jnp.where(causal_blk, imp, -jnp.inf))
        _, sel = lax.top_k(imp, k_top)
        k_g = ksb[bi, sel]
        v_g = vsb[bi, sel]
        kpos = sel[..., None] * NSA_SEL_BLOCK + in_blk
        mask_s = (sel <= qblk[None, :, None])[..., None] & (kpos <= qpos[None, :, None, None])
        s_s = jnp.einsum('bhqd,bqnld->bhqnl', qrb, k_g).astype(jnp.float32) * scale
        p_s = masked_softmax(s_s.reshape(B, H, Q_BLOCK, -1), mask_s.reshape(B, 1, Q_BLOCK, -1))
        o_s = jnp.einsum('bhqnl,bqnld->bhqd', p_s.reshape(s_s.shape).astype(v_g.dtype), v_g)
        k_w = lax.dynamic_slice_in_dim(kw_pad, q0, NSA_WINDOW + Q_BLOCK, axis=1)
        v_w = lax.dynamic_slice_in_dim(vw_pad, q0, NSA_WINDOW + Q_BLOCK, axis=1)
        kpos_w = q0 - NSA_WINDOW + win_off
        mask_w = ((kpos_w[None, :] >= 0) & (kpos_w[None, :] <= qpos[:, None])
                  & (kpos_w[None, :] > qpos[:, None] - NSA_WINDOW))
        s_w = jnp.einsum('bhqd,bkd->bhqk', qrb, k_w).astype(jnp.float32) * scale
        p_w = masked_softmax(s_w, mask_w)
        o_w = jnp.einsum('bhqk,bkd->bhqd', p_w.astype(v_w.dtype), v_w)
        return gb[..., 0:1] * o_c + gb[..., 1:2] * o_s + gb[..., 2:3] * o_w

    qt = q.transpose(0, 2, 1, 3)
    qrt = qr.transpose(0, 2, 1, 3)
    gt = gates.transpose(0, 2, 1, 3)
    out = lax.map(block, (to_qblocks(qt, Q_BLOCK), to_qblocks(qrt, Q_BLOCK), to_qblocks(gt, Q_BLOCK),
                          jnp.arange(S // Q_BLOCK)))
    return from_qblocks(out)


def token_mixer(h, positions, w_in, w_out, diff_lambda, diff_gain, cmp_pos, cmp_w1, cmp_w2, layer_idx):
    B, S, _ = h.shape
    H = N_GROUP_HEADS
    proj = jnp.einsum('bsd,dn->bsn', h, w_in)
    points = [int(v) for v in np.cumsum(IN_SPLITS)[:-1]]
    (a_q, a_k, a_v, b_q, b_k, b_v, c_q, c_k, c_v, d_q,
     d_kc, d_vc, d_ks, d_vs, d_kw, d_vw, d_g) = jnp.split(proj, points, axis=-1)
    o_a = stick_breaking_attention(split_heads(a_q, H), split_heads(a_k, H), split_heads(a_v, H))
    bq = rope(b_q.reshape(B, S, 2 * H, DIFF_HALF), positions).reshape(B, S, H, 2, DIFF_HALF).transpose(0, 2, 3, 1, 4)
    bk = rope(b_k.reshape(B, S, 2 * H, DIFF_HALF), positions).reshape(B, S, H, 2, DIFF_HALF).transpose(0, 2, 3, 1, 4)
    lam_init = 0.8 - 0.6 * math.exp(-0.3 * layer_idx)
    lf = diff_lambda.astype(jnp.float32)
    lam = jnp.exp(jnp.sum(lf[0] * lf[1])) - jnp.exp(jnp.sum(lf[2] * lf[3])) + lam_init
    o_b = diff_attention(bq, bk, split_heads(b_v, H), lam, diff_gain, lam_init)
    cq = rope(c_q.reshape(B, S, H, HEAD_DIM), positions).transpose(0, 2, 1, 3)
    ck = rope(c_k.reshape(B, S, H, HEAD_DIM), positions).transpose(0, 2, 1, 3)
    o_c = moba_attention(cq, ck, split_heads(c_v, H))
    dq = d_q.reshape(B, S, H, HEAD_DIM)
    ks_r = rope(d_ks.reshape(B, S, 1, HEAD_DIM), positions).reshape(B, S, HEAD_DIM)
    kw_r = rope(d_kw.reshape(B, S, 1, HEAD_DIM), positions).reshape(B, S, HEAD_DIM)
    gates = jax.nn.sigmoid(d_g.astype(jnp.float32)).astype(h.dtype).reshape(B, S, H, NSA_N_BRANCH)
    o_d = nsa_attention(dq, rope(dq, positions), d_kc, d_vc, ks_r, d_vs, kw_r, d_vw, gates, cmp_pos, cmp_w1, cmp_w2)
    mixed = jnp.concatenate([merge_heads(o_a), merge_heads(o_b), merge_heads(o_c), merge_heads(o_d)], -1)
    return jnp.einsum('bsn,nd->bsd', mixed, w_out)


def swiglu(x, w1, w3, w2):
    return (jax.nn.silu(x @ w1) * (x @ w3)) @ w2


def moe_swiglu(x, w_router, w1, w3, w2):
    B, S, D = x.shape
    T = B * S
    xf = x.reshape(T, D)
    logits = (xf @ w_router).astype(jnp.float32)
    top_logits, top_e = lax.top_k(logits, TOP_K)
    gates = jax.nn.softmax(top_logits, axis=-1).astype(x.dtype)
    flat_e = top_e.reshape(-1)
    order = jnp.argsort(flat_e)
    e_sorted = flat_e[order]
    tok_sorted = order // TOP_K
    gate_sorted = gates.reshape(-1)[order]
    counts = jnp.bincount(flat_e, length=N_EXPERTS)
    padded = (counts + MOE_ROW_BLOCK - 1) // MOE_ROW_BLOCK * MOE_ROW_BLOCK
    start = jnp.cumsum(counts) - counts
    pad_end = jnp.cumsum(padded)
    pad_start = pad_end - padded
    dest = pad_start[e_sorted] + jnp.arange(T * TOP_K) - start[e_sorted]
    n_rows = T * TOP_K + N_EXPERTS * MOE_ROW_BLOCK
    n_blk = n_rows // MOE_ROW_BLOCK
    row_tok = jnp.zeros((n_rows,), jnp.int32).at[dest].set(tok_sorted)
    blk_e = jnp.minimum(jnp.searchsorted(pad_end, jnp.arange(n_blk) * MOE_ROW_BLOCK, side='right'), N_EXPERTS - 1)
    xb = xf[row_tok].reshape(n_blk, MOE_ROW_BLOCK, D)

    def expert_block(args):
        xblk, e = args
        return swiglu(xblk, w1[e], w3[e], w2[e])

    y = lax.map(expert_block, (xb, blk_e)).reshape(n_rows, D)
    out = jnp.zeros((T, D), x.dtype).at[tok_sorted].add(y[dest] * gate_sorted[:, None])
    return out.reshape(B, S, D)


def setup_inputs(seed: int = 0) -> dict:
    key = jax.random.key(seed)
    ks = jax.random.split(key, 24)
    f32 = jnp.float32
    n_dense = (DEPTH + 1) // 2
    n_moe = DEPTH // 2

    def nrm(k, shape, scale):
        return jax.random.normal(k, shape, f32) * scale

    offsets = jax.random.randint(ks[2], (BATCH, 1), 0, MAX_POS_OFFSET, dtype=jnp.int32)
    return {
        'x': nrm(ks[0], (BATCH, SEQ, D_MODEL), 1.0),
        'p': nrm(ks[1], (DEPTH, BATCH, SEQ, PLE_DIM), 1.0),
        'positions': offsets + jnp.arange(SEQ, dtype=jnp.int32)[None, :],
        'w_in': nrm(ks[3], (DEPTH, D_MODEL, N_IN), D_MODEL ** -0.5),
        'w_out': nrm(ks[4], (DEPTH, D_MODEL, D_MODEL), D_MODEL ** -0.5 * DEEPNORM_BETA),
        'ln_mix_g': 1.0 + nrm(ks[5], (DEPTH, D_MODEL), 0.02),
        'ln_mix_b': nrm(ks[6], (DEPTH, D_MODEL), 0.02),
        'diff_lambda': nrm(ks[7], (DEPTH, 4, DIFF_HALF), 0.1),
        'diff_gain': 1.0 + nrm(ks[8], (DEPTH, HEAD_DIM), 0.02),
        'nsa_cmp_pos': nrm(ks[9], (DEPTH, 2, NSA_CMP_LEN, HEAD_DIM), 0.1),
        'nsa_cmp_w1': nrm(ks[10], (DEPTH, 2, NSA_CMP_LEN * HEAD_DIM, NSA_CMP_HIDDEN), (NSA_CMP_LEN * HEAD_DIM) ** -0.5),
        'nsa_cmp_w2': nrm(ks[11], (DEPTH, 2, NSA_CMP_HIDDEN, HEAD_DIM), NSA_CMP_HIDDEN ** -0.5),
        'ffn_w1': nrm(ks[12], (n_dense, D_MODEL, D_FF), D_MODEL ** -0.5),
        'ffn_w3': nrm(ks[13], (n_dense, D_MODEL, D_FF), D_MODEL ** -0.5),
        'ffn_w2': nrm(ks[14], (n_dense, D_FF, D_MODEL), D_FF ** -0.5 * DEEPNORM_BETA),
        'moe_router': nrm(ks[15], (n_moe, D_MODEL, N_EXPERTS), D_MODEL ** -0.5),
        'moe_w1': nrm(ks[16], (n_moe, N_EXPERTS, D_MODEL, D_FF), D_MODEL ** -0.5),
        'moe_w3': nrm(ks[17], (n_moe, N_EXPERTS, D_MODEL, D_FF), D_MODEL ** -0.5),
        'moe_w2': nrm(ks[18], (n_moe, N_EXPERTS, D_FF, D_MODEL), D_FF ** -0.5 * DEEPNORM_BETA),
        'ln_ffn_g': 1.0 + nrm(ks[19], (DEPTH, D_MODEL), 0.02),
        'ln_ffn_b': nrm(ks[20], (DEPTH, D_MODEL), 0.02),
        'ple_proj': nrm(ks[21], (DEPTH, PLE_DIM, D_MODEL), PLE_DIM ** -0.5),
        'ple_gate': nrm(ks[22], (DEPTH, D_MODEL, D_MODEL), D_MODEL ** -0.5),
    }


def reference(x, p, positions, w_in, w_out, ln_mix_g, ln_mix_b, diff_lambda, diff_gain,
              nsa_cmp_pos, nsa_cmp_w1, nsa_cmp_w2, ffn_w1, ffn_w3, ffn_w2,
              moe_router, moe_w1, moe_w3, moe_w2, ln_ffn_g, ln_ffn_b, ple_proj, ple_gate):
    h = x
    for i in range(DEPTH):
        mix = token_mixer(h, positions, w_in[i], w_out[i], diff_lambda[i], diff_gain[i],
                          nsa_cmp_pos[i], nsa_cmp_w1[i], nsa_cmp_w2[i], i)
        h = layer_norm(DEEPNORM_ALPHA * h + mix, ln_mix_g[i], ln_mix_b[i])
        if i % 2 == 0:
            f = swiglu(h, ffn_w1[i // 2], ffn_w3[i // 2], ffn_w2[i // 2])
        else:
            f = moe_swiglu(h, moe_router[i // 2], moe_w1[i // 2], moe_w3[i // 2], moe_w2[i // 2])
        h = layer_norm(DEEPNORM_ALPHA * h + f, ln_ffn_g[i], ln_ffn_b[i])
        h = h + jax.nn.sigmoid(h @ ple_gate[i]) * (p[i] @ ple_proj[i])
    return h
```

```python
import functools
import math

import jax
import jax.numpy as jnp
from jax import lax
from jax.experimental import pallas as pl
from jax.experimental.pallas import tpu as pltpu

F32 = jnp.float32
BF16 = jnp.bfloat16
HIGHEST = lax.Precision.HIGHEST

LANES = 128
HEAD_DIM = 64
GROUP_WIDTH = 256
N_HEADS = 4
DIFF_HALF = 32
ROPE_THETA = 10000.0
MOBA_BLOCK = 256
MOBA_TOPK = 3
CMP_LEN = 32
CMP_STRIDE = 16
SEL_BLOCK = 64
SEL_TOPK = 16
WINDOW = 512
N_EXPERTS = 8
LN_EPS = 1e-5
RMS_EPS = 1e-5
NEG = -1e30
ATT_TILE = 256
MOE_ROWS = 512
VMEM_LIMIT = 48 * 1024 * 1024

BLK_AQ, BLK_AK, BLK_AV = 0, 2, 4
BLK_BQ, BLK_BK, BLK_BV = 6, 8, 10
BLK_CQ, BLK_CK, BLK_CV = 12, 14, 16
BLK_DQ, BLK_KVC, BLK_KVS, BLK_KVW, BLK_G = 18, 20, 21, 22, 23
N_IN_PAD = 24 * LANES
ROPE_SRC = (BLK_BQ, BLK_BQ + 1, BLK_BK, BLK_BK + 1,
            BLK_CQ, BLK_CQ + 1, BLK_CK, BLK_CK + 1, BLK_DQ, BLK_DQ + 1, BLK_KVS, BLK_KVW)
ROPE_KIND = (1, 1, 1, 1, 0, 0, 0, 0, 0, 0, 2, 2)
RP_BQ, RP_BK, RP_CQ, RP_CK, RP_DQ, RP_KVS, RP_KVW = 0, 2, 4, 6, 8, 10, 11


def _params(*sem):
    return pltpu.CompilerParams(dimension_semantics=sem, vmem_limit_bytes=VMEM_LIMIT)


def _dot(a, b, precision=None):
    return jnp.dot(a, b, preferred_element_type=F32, precision=precision)


def _dot_nt(a, b, precision=None):
    return lax.dot_general(a, b, (((1,), (1,)), ((), ())),
                           preferred_element_type=F32, precision=precision)


def _dot_split(a, b_exact, pieces):
    out = None
    rest = a
    for _ in range(pieces):
        part = rest.astype(BF16)
        rest = rest - part.astype(F32)
        term = _dot(part, b_exact)
        out = term if out is None else out + term
    return out


def _layer_norm(x, g, b):
    mu = jnp.mean(x, axis=-1, keepdims=True)
    xc = x - mu
    var = jnp.mean(xc * xc, axis=-1, keepdims=True)
    return xc * lax.rsqrt(var + LN_EPS) * g + b


def _softmax_step(s, m, l, acc, v):
    m_new = jnp.maximum(m, jnp.max(s, axis=-1, keepdims=True))
    alpha = jnp.exp(m - m_new)
    p = jnp.exp(s - m_new)
    l = alpha * l + jnp.sum(p, axis=-1, keepdims=True)
    acc = alpha * acc + _dot(p.astype(BF16), v)
    return m_new, l, acc


def _matmul_kernel(x_ref, w_ref, o_ref, xb_ref):
    @pl.when(pl.program_id(1) == 0)
    def _():
        xb_ref[...] = x_ref[...].astype(BF16)

    o_ref[...] = _dot(xb_ref[...], w_ref[...])


def _matmul(x, w, tm, tn):
    m, k = x.shape
    n = w.shape[1]
    return pl.pallas_call(
        _matmul_kernel,
        out_shape=jax.ShapeDtypeStruct((m, n), F32),
        grid=(m // tm, n // tn),
        in_specs=[pl.BlockSpec((tm, k), lambda i, j: (i, 0)),
                  pl.BlockSpec((k, tn), lambda i, j: (0, j))],
        out_specs=pl.BlockSpec((tm, tn), lambda i, j: (i, j)),
        scratch_shapes=[pltpu.VMEM((tm, k), BF16)],
        compiler_params=_params("parallel", "arbitrary"),
        name="in_proj",
    )(x, w)


def _rope_kernel(src_ref, kind_ref, x_ref, c_ref, s32_ref, s16_ref, o_ref, mean_ref):
    x = x_ref[...]
    lane = lax.broadcasted_iota(jnp.int32, x.shape, 1)
    swap32 = jnp.where((lane & 32) == 0, pltpu.roll(x, LANES - 32, 1), pltpu.roll(x, 32, 1))
    swap16 = jnp.where((lane & 16) == 0, pltpu.roll(x, LANES - 16, 1), pltpu.roll(x, 16, 1))
    y = x * c_ref[0] + swap32 * s32_ref[0] + swap16 * s16_ref[0]
    o_ref[...] = y
    mean_ref[0] = jnp.mean(y, axis=0, keepdims=True)


def _rope(proj, cos_t, sin32_t, sin16_t):
    t = proj.shape[0]
    tm = MOBA_BLOCK
    n_out = len(ROPE_SRC)
    src = jnp.asarray(ROPE_SRC, jnp.int32)
    kind = jnp.asarray(ROPE_KIND, jnp.int32)
    tab_spec = pl.BlockSpec((1, tm, LANES), lambda i, j, src, kind: (kind[j], i, 0))
    return pl.pallas_call(
        _rope_kernel,
        out_shape=(jax.ShapeDtypeStruct((t, n_out * LANES), F32),
                   jax.ShapeDtypeStruct((t // tm, 1, n_out * LANES), F32)),
        grid_spec=pltpu.PrefetchScalarGridSpec(
            num_scalar_prefetch=2,
            grid=(t // tm, n_out),
            in_specs=[pl.BlockSpec((tm, LANES), lambda i, j, src, kind: (i, src[j])),
                      tab_spec, tab_spec, tab_spec],
            out_specs=(pl.BlockSpec((tm, LANES), lambda i, j, src, kind: (i, j)),
                       pl.BlockSpec((1, 1, LANES), lambda i, j, src, kind: (i, 0, j)))),
        compiler_params=_params("parallel", "arbitrary"),
        name="rope",
    )(src, kind, proj, cos_t, sin32_t, sin16_t)


def _rope_tables(positions):
    pos = positions.astype(F32).reshape(-1, 1)
    lane = jnp.arange(LANES)

    def tables(half, active):
        inv_freq = ROPE_THETA ** (-jnp.arange(half, dtype=F32) / half)
        ang = pos * inv_freq[lane % half][None, :]
        sign = jnp.where((lane & half) == 0, -1.0, 1.0)
        cos = jnp.where(active, jnp.cos(ang), 1.0)
        sin = jnp.where(active, jnp.sin(ang) * sign, 0.0)
        return cos, sin

    c0, s0 = tables(32, lane >= 0)
    c1, s1 = tables(16, lane >= 0)
    c2, s2 = tables(32, lane < HEAD_DIM)
    zero = jnp.zeros_like(s0)
    return (jnp.stack([c0, c1, c2]), jnp.stack([s0, zero, s2]), jnp.stack([zero, s1, zero]))


def _sb_kernel(q_ref, kt_ref, v_ref, tri_ref, o_ref):
    tq = q_ref.shape[1]
    qi = pl.program_id(2)
    q = (q_ref[0] * HEAD_DIM ** -0.5).astype(BF16)
    tri = tri_ref[...]
    row = lax.broadcasted_iota(jnp.int32, (tq, tq), 0)
    col = lax.broadcasted_iota(jnp.int32, (tq, tq), 1)
    past = col < row
    lane = lax.broadcasted_iota(jnp.int32, (tq, LANES), 1)
    out = jnp.zeros((tq, LANES), F32)
    for h in range(2):
        qh = q[:, HEAD_DIM * h:HEAD_DIM * (h + 1)]

        def tile(t, carry, diag, h=h, qh=qh):
            acc, run = carry
            z = _dot(qh, kt_ref[0, 0, t, HEAD_DIM * h:HEAD_DIM * (h + 1), :])
            lg = -(jnp.maximum(z, 0.0) + jnp.log1p(jnp.exp(-jnp.abs(z))))
            if diag:
                lg = jnp.where(past, lg, 0.0)
            cum = _dot_split(lg, tri, 2)
            w = jnp.exp(z + cum + run)
            if diag:
                w = jnp.where(past, w, 0.0)
            vt = v_ref[0, pl.ds(pl.multiple_of(t * tq, tq), tq), :]
            return acc + _dot(w.astype(BF16), vt), run + cum[:, 0:1]

        carry = tile(qi, (jnp.zeros((tq, LANES), F32), jnp.zeros((tq, 1), F32)), True)
        carry = lax.fori_loop(0, qi, lambda s, c: tile(qi - 1 - s, c, False), carry)
        out = jnp.where((lane >= HEAD_DIM * h) & (lane < HEAD_DIM * (h + 1)), carry[0], out)
    o_ref[0] = out.astype(o_ref.dtype)


def _kt_tiles(k, tk):
    b, s, c = k.shape
    k = k.astype(BF16).reshape(b, s // tk, tk, c // LANES, LANES)
    return k.transpose(0, 3, 1, 4, 2)


def _stick_breaking(q_src, q_blk, kt, v):
    b, s, _ = q_src.shape
    tq = min(ATT_TILE, s)
    nt = s // tq
    tri = (jnp.arange(tq)[:, None] >= jnp.arange(tq)[None, :]).astype(BF16)
    return pl.pallas_call(
        _sb_kernel,
        out_shape=jax.ShapeDtypeStruct((b, s, GROUP_WIDTH), BF16),
        grid=(b, 2, nt),
        in_specs=[pl.BlockSpec((1, tq, LANES), lambda bi, hp, qi: (bi, qi, q_blk + hp)),
                  pl.BlockSpec((1, 1, nt, LANES, tq), lambda bi, hp, qi: (bi, hp, 0, 0, 0)),
                  pl.BlockSpec((1, s, LANES), lambda bi, hp, qi: (bi, 0, hp)),
                  pl.BlockSpec((tq, tq), lambda bi, hp, qi: (0, 0))],
        out_specs=pl.BlockSpec((1, tq, LANES), lambda bi, hp, qi: (bi, qi, hp)),
        compiler_params=_params("parallel", "parallel", "arbitrary"),
        name="stick_breaking",
    )(q_src, kt, v, tri)


def _diff_kernel(q_ref, kt_ref, v_ref, lam_ref, gain_ref, o_ref, *, lam_init):
    tq = q_ref.shape[1]
    qi = pl.program_id(2)
    q = q_ref[0].astype(BF16)
    scale = DIFF_HALF ** -0.5
    row = lax.broadcasted_iota(jnp.int32, (tq, tq), 0)
    col = lax.broadcasted_iota(jnp.int32, (tq, tq), 1)
    causal = col <= row
    lane = lax.broadcasted_iota(jnp.int32, (tq, LANES), 1)
    lf = lam_ref[...]
    lam = (jnp.exp(jnp.sum(lf[0:1] * lf[1:2], axis=-1, keepdims=True))
           - jnp.exp(jnp.sum(lf[2:3] * lf[3:4], axis=-1, keepdims=True)) + lam_init)
    out = jnp.zeros((tq, LANES), F32)
    for h in range(2):
        comps = []
        for c in range(2):
            lo = HEAD_DIM * h + DIFF_HALF * c
            qc = q[:, lo:lo + DIFF_HALF]

            def tile(t, carry, diag, lo=lo, qc=qc):
                s = _dot(qc, kt_ref[0, 0, t, lo:lo + DIFF_HALF, :]) * scale
                if diag:
                    s = jnp.where(causal, s, NEG)
                vt = v_ref[0, pl.ds(pl.multiple_of(t * tq, tq), tq), :]
                return _softmax_step(s, *carry, vt)

            init = (jnp.full((tq, 1), NEG, F32), jnp.zeros((tq, 1), F32), jnp.zeros((tq, LANES), F32))
            m, l, acc = tile(qi, init, True)
            m, l, acc = lax.fori_loop(0, qi, lambda s_, cr: tile(qi - 1 - s_, cr, False), (m, l, acc))
            comps.append(acc * (1.0 / l))
        o = comps[0] - lam * comps[1]
        head = (lane >= HEAD_DIM * h) & (lane < HEAD_DIM * (h + 1))
        ms = jnp.sum(jnp.where(head, o * o, 0.0), axis=-1, keepdims=True) * (1.0 / HEAD_DIM)
        o = o * lax.rsqrt(ms + RMS_EPS) * gain_ref[...] * (1.0 - lam_init)
        out = jnp.where(head, o, out)
    o_ref[0] = out.astype(o_ref.dtype)


def _diff_attention(q_src, q_blk, kt, v, lam_params, gain2, lam_init):
    b, s, _ = q_src.shape
    tq = min(ATT_TILE, s)
    nt = s // tq
    return pl.pallas_call(
        functools.partial(_diff_kernel, lam_init=lam_init),
        out_shape=jax.ShapeDtypeStruct((b, s, GROUP_WIDTH), BF16),
        grid=(b, 2, nt),
        in_specs=[pl.BlockSpec((1, tq, LANES), lambda bi, hp, qi: (bi, qi, q_blk + hp)),
                  pl.BlockSpec((1, 1, nt, LANES, tq), lambda bi, hp, qi: (bi, hp, 0, 0, 0)),
                  pl.BlockSpec((1, s, LANES), lambda bi, hp, qi: (bi, 0, hp)),
                  pl.BlockSpec((4, DIFF_HALF), lambda bi, hp, qi: (0, 0)),
                  pl.BlockSpec((1, LANES), lambda bi, hp, qi: (0, 0))],
        out_specs=pl.BlockSpec((1, tq, LANES), lambda bi, hp, qi: (bi, qi, hp)),
        compiler_params=_params("parallel", "parallel", "arbitrary"),
        name="diff_attention",
    )(q_src, kt, v, lam_params, gain2)


def _topk_mask(work, k):
    lane = lax.broadcasted_iota(jnp.int32, work.shape, 1)
    sel = jnp.zeros(work.shape, jnp.bool_)
    for _ in range(k):
        mx = jnp.max(work, axis=-1, keepdims=True)
        idx = jnp.min(jnp.where(work == mx, lane, LANES), axis=-1, keepdims=True)
        pick = lane == idx
        sel = sel | pick
        work = jnp.where(pick, -jnp.inf, work)
    return sel


def _moba_kernel(q_ref, kmean_ref, kt_ref, v_ref, exp_ref, o_ref):
    tq = q_ref.shape[1]
    qi = pl.program_id(2)
    qf = q_ref[0]
    q = (qf * HEAD_DIM ** -0.5).astype(BF16)
    row = lax.broadcasted_iota(jnp.int32, (tq, tq), 0)
    col = lax.broadcasted_iota(jnp.int32, (tq, tq), 1)
    causal = col <= row
    lane = lax.broadcasted_iota(jnp.int32, (tq, LANES), 1)
    out = jnp.zeros((tq, LANES), F32)
    for h in range(2):
        sl = slice(HEAD_DIM * h, HEAD_DIM * (h + 1))
        gate = _dot_nt(qf[:, sl], kmean_ref[0, :, sl], HIGHEST)
        past_blk = lane < qi
        allowed = _topk_mask(jnp.where(past_blk, gate, -1e38), MOBA_TOPK) & past_blk
        allowed = jnp.where(allowed, 1.0, 0.0).astype(BF16)
        qh = q[:, sl]

        def tile(t, carry, diag, sl=sl, qh=qh, allowed=allowed):
            s = _dot(qh, kt_ref[0, 0, t, sl, :])
            if diag:
                s = jnp.where(causal, s, NEG)
            else:
                s = jnp.where(_dot(allowed, exp_ref[t]) > 0.5, s, NEG)
            vt = v_ref[0, pl.ds(pl.multiple_of(t * tq, tq), tq), :]
            return _softmax_step(s, *carry, vt)

        init = (jnp.full((tq, 1), NEG, F32), jnp.zeros((tq, 1), F32), jnp.zeros((tq, LANES), F32))
        m, l, acc = tile(qi, init, True)
        m, l, acc = lax.fori_loop(0, qi, lambda s_, cr: tile(qi - 1 - s_, cr, False), (m, l, acc))
        out = jnp.where((lane >= sl.start) & (lane < sl.stop), acc * (1.0 / l), out)
    o_ref[0] = out.astype(o_ref.dtype)


def _block_expander(n_keys, block, tk):
    key_blk = jnp.arange(n_keys) // block
    e = (jnp.arange(LANES)[:, None] == key_blk[None, :]).astype(BF16)
    return e.reshape(LANES, n_keys // tk, tk).transpose(1, 0, 2)


def _moba(rp, kmean, kt, v):
    b, s, _ = rp.shape
    tq = MOBA_BLOCK
    nt = s // tq
    expander = _block_expander(s, MOBA_BLOCK, tq)
    return pl.pallas_call(
        _moba_kernel,
        out_shape=jax.ShapeDtypeStruct((b, s, GROUP_WIDTH), BF16),
        grid=(b, 2, nt),
        in_specs=[pl.BlockSpec((1, tq, LANES), lambda bi, hp, qi: (bi, qi, RP_CQ + hp)),
                  pl.BlockSpec((1, LANES, LANES), lambda bi, hp, qi: (bi, 0, hp)),
                  pl.BlockSpec((1, 1, nt, LANES, tq), lambda bi, hp, qi: (bi, hp, 0, 0, 0)),
                  pl.BlockSpec((1, s, LANES), lambda bi, hp, qi: (bi, 0, hp)),
                  pl.BlockSpec((nt, LANES, tq), lambda bi, hp, qi: (0, 0, 0))],
        out_specs=pl.BlockSpec((1, tq, LANES), lambda bi, hp, qi: (bi, qi, hp)),
        compiler_params=_params("parallel", "parallel", "arbitrary"),
        name="moba",
    )(rp, kmean, kt, v, expander)


def _compress_kernel(c_ref, pe_ref, w1_ref, w2_ref, o_ref):
    c = c_ref[0, 0]
    n = c.shape[0]
    y0 = _dot(c + pe_ref[0, 0:1], w1_ref[0, 0], HIGHEST)
    y1 = _dot(c + pe_ref[0, 1:2], w1_ref[0, 1], HIGHEST)
    pre = y0 + pltpu.roll(y1, n - 1, 0)
    hid = pre * (1.0 / (1.0 + jnp.exp(-pre)))
    o_ref[0, 0] = _dot(hid, w2_ref[0], HIGHEST)


def _compress(chunks, pe, w1, w2):
    _, b, n, width = chunks.shape
    hidden = w1.shape[-1]
    return pl.pallas_call(
        _compress_kernel,
        out_shape=jax.ShapeDtypeStruct((2, b, n, HEAD_DIM), F32),
        grid=(2, b),
        in_specs=[pl.BlockSpec((1, 1, n, width), lambda j, bi: (j, bi, 0, 0)),
                  pl.BlockSpec((1, 2, width), lambda j, bi: (j, 0, 0)),
                  pl.BlockSpec((1, 2, width, hidden), lambda j, bi: (j, 0, 0, 0)),
                  pl.BlockSpec((1, hidden, HEAD_DIM), lambda j, bi: (j, 0, 0))],
        out_specs=pl.BlockSpec((1, 1, n, HEAD_DIM), lambda j, bi: (j, bi, 0, 0)),
        compiler_params=_params("parallel", "parallel"),
        name="nsa_compress",
    )(chunks, pe, w1, w2)


def _nsa_cmp_kernel(q_ref, kc_ref, vc_ref, ov_ref, oc_ref, sel_ref):
    tq = q_ref.shape[1]
    n_cmp = kc_ref.shape[1]
    qi = pl.program_id(1)
    q = q_ref[0]
    kc = kc_ref[0]
    vc4 = vc_ref[0].astype(BF16)
    qpos = qi * tq + lax.broadcasted_iota(jnp.int32, (tq, 1), 0)
    cmp_end = lax.broadcasted_iota(jnp.int32, (1, n_cmp), 1) * CMP_STRIDE + (CMP_LEN - 1)
    valid = cmp_end <= qpos
    lane4 = lax.broadcasted_iota(jnp.int32, (tq, GROUP_WIDTH), 1)
    p_sum = jnp.zeros((tq, n_cmp), F32)
    out = jnp.zeros((tq, GROUP_WIDTH), F32)
    for h in range(N_HEADS):
        s = _dot_nt(q[:, HEAD_DIM * h:HEAD_DIM * (h + 1)], kc, HIGHEST) * HEAD_DIM ** -0.5
        m = jnp.max(jnp.where(valid, s, NEG), axis=-1, keepdims=True)
        e = jnp.where(valid, jnp.exp(s - m), 0.0)
        den = jnp.sum(e, axis=-1, keepdims=True)
        p = e / jnp.where(den > 0.0, den, 1.0)
        p_sum = p_sum + p
        o = _dot(p.astype(BF16), vc4)
        out = jnp.where((lane4 >= HEAD_DIM * h) & (lane4 < HEAD_DIM * (h + 1)), o, out)
    oc_ref[0] = out
    imp = _dot_split(p_sum, ov_ref[...], 3)
    blk = lax.broadcasted_iota(jnp.int32, (tq, LANES), 1)
    qblk = qpos // SEL_BLOCK
    forced = (blk == 0) | (blk == qblk) | (blk == qblk - 1)
    work = jnp.where(forced, 1e30, jnp.where(blk <= qblk, imp, -1.0))
    sel = _topk_mask(work, SEL_TOPK) & (blk <= qblk)
    sel_ref[0] = jnp.where(sel, 1.0, 0.0).astype(BF16)


def _nsa_cmp(proj, kc, vc4, overlap):
    b, s, _ = proj.shape
    tq = min(ATT_TILE, s)
    n_cmp = kc.shape[1]
    return pl.pallas_call(
        _nsa_cmp_kernel,
        out_shape=(jax.ShapeDtypeStruct((b, s, GROUP_WIDTH), F32),
                   jax.ShapeDtypeStruct((b, s, LANES), BF16)),
        grid=(b, s // tq),
        in_specs=[pl.BlockSpec((1, tq, GROUP_WIDTH), lambda bi, qi: (bi, qi, BLK_DQ // 2)),
                  pl.BlockSpec((1, n_cmp, HEAD_DIM), lambda bi, qi: (bi, 0, 0)),
                  pl.BlockSpec((1, n_cmp, GROUP_WIDTH), lambda bi, qi: (bi, 0, 0)),
                  pl.BlockSpec((n_cmp, LANES), lambda bi, qi: (0, 0))],
        out_specs=(pl.BlockSpec((1, tq, GROUP_WIDTH), lambda bi, qi: (bi, qi, 0)),
                   pl.BlockSpec((1, tq, LANES), lambda bi, qi: (bi, qi, 0))),
        compiler_params=_params("parallel", "arbitrary"),
        name="nsa_compressed",
    )(proj, kc, vc4, overlap)


def _nsa_main_kernel(q_ref, kst_ref, vs_ref, kwt_ref, vw_ref, sel_ref, exp_ref, oc_ref, g_ref, o_ref):
    tq = q_ref.shape[1]
    qi = pl.program_id(1)
    q = (q_ref[0] * HEAD_DIM ** -0.5).astype(BF16)
    sel = sel_ref[0]
    row = lax.broadcasted_iota(jnp.int32, (tq, tq), 0)
    col = lax.broadcasted_iota(jnp.int32, (tq, tq), 1)
    qh = [q[:, HEAD_DIM * h:HEAD_DIM * (h + 1)] for h in range(N_HEADS)]

    def init():
        return tuple((jnp.full((tq, 1), NEG, F32), jnp.zeros((tq, 1), F32), jnp.zeros((tq, LANES), F32))
                     for _ in range(N_HEADS))

    def sel_tile(t, carry, diag):
        ok = _dot(sel, exp_ref[t]) > 0.5
        if diag:
            ok = ok & (col <= row)
        kt = kst_ref[0, t]
        vt = vs_ref[0, pl.ds(pl.multiple_of(t * tq, tq), tq), :]
        return tuple(_softmax_step(jnp.where(ok, _dot(qh[h], kt), NEG), *carry[h], vt)
                     for h in range(N_HEADS))

    st = sel_tile(qi, init(), True)
    st = lax.fori_loop(0, qi, lambda s_, cr: sel_tile(qi - 1 - s_, cr, False), st)

    def win_tile(t, carry):
        delta = (qi - t) * tq + row - col
        ok = (delta >= 0) & (delta < WINDOW)
        kt = kwt_ref[0, t]
        vt = vw_ref[0, pl.ds(pl.multiple_of(t * tq, tq), tq), :]
        return tuple(_softmax_step(jnp.where(ok, _dot(qh[h], kt), NEG), *carry[h], vt)
                     for h in range(N_HEADS))

    n_back = (WINDOW + tq - 1) // tq
    wt = win_tile(qi, init())
    wt = lax.fori_loop(0, jnp.minimum(qi, n_back), lambda s_, cr: win_tile(qi - 1 - s_, cr), wt)

    g = 1.0 / (1.0 + jnp.exp(-g_ref[0]))
    oc = oc_ref[0]
    lane = lax.broadcasted_iota(jnp.int32, (tq, LANES), 1)
    for j in range(2):
        halves = []
        for h in (2 * j, 2 * j + 1):
            o_s = st[h][2] * (1.0 / st[h][1])
            o_w = wt[h][2] * (1.0 / wt[h][1])
            halves.append(g[:, 3 * h + 1:3 * h + 2] * o_s + g[:, 3 * h + 2:3 * h + 3] * o_w)
        both = jnp.where(lane < HEAD_DIM, halves[0], halves[1])
        g_c = jnp.where(lane < HEAD_DIM, g[:, 6 * j:6 * j + 1], g[:, 6 * j + 3:6 * j + 4])
        o_ref[0, :, LANES * j:LANES * (j + 1)] = (both + g_c * oc[:, LANES * j:LANES * (j + 1)]).astype(o_ref.dtype)


def _nsa_main(rp, proj, kst, vs2, kwt, vw2, selmask, o_cmp):
    b, s, _ = rp.shape
    tq = min(ATT_TILE, s)
    nt = s // tq
    expander = _block_expander(s, SEL_BLOCK, tq)
    return pl.pallas_call(
        _nsa_main_kernel,
        out_shape=jax.ShapeDtypeStruct((b, s, GROUP_WIDTH), BF16),
        grid=(b, nt),
        in_specs=[pl.BlockSpec((1, tq, GROUP_WIDTH), lambda bi, qi: (bi, qi, RP_DQ // 2)),
                  pl.BlockSpec((1, nt, HEAD_DIM, tq), lambda bi, qi: (bi, 0, 0, 0)),
                  pl.BlockSpec((1, s, LANES), lambda bi, qi: (bi, 0, 0)),
                  pl.BlockSpec((1, nt, HEAD_DIM, tq), lambda bi, qi: (bi, 0, 0, 0)),
                  pl.BlockSpec((1, s, LANES), lambda bi, qi: (bi, 0, 0)),
                  pl.BlockSpec((1, tq, LANES), lambda bi, qi: (bi, qi, 0)),
                  pl.BlockSpec((nt, LANES, tq), lambda bi, qi: (0, 0, 0)),
                  pl.BlockSpec((1, tq, GROUP_WIDTH), lambda bi, qi: (bi, qi, 0)),
                  pl.BlockSpec((1, tq, LANES), lambda bi, qi: (bi, qi, BLK_G))],
        out_specs=pl.BlockSpec((1, tq, GROUP_WIDTH), lambda bi, qi: (bi, qi, 0)),
        compiler_params=_params("parallel", "arbitrary"),
        name="nsa_selected_window",
    )(rp, kst, vs2, kwt, vw2, selmask, expander, o_cmp, proj)


def _out_proj_kernel(oa_ref, ob_ref, oc_ref, od_ref, w_ref, h_ref, g_ref, b_ref, o_ref, *, alpha):
    mix = None
    for i, ref in enumerate((oa_ref, ob_ref, oc_ref, od_ref)):
        term = _dot(ref[...], w_ref[GROUP_WIDTH * i:GROUP_WIDTH * (i + 1), :])
        mix = term if mix is None else mix + term
    o_ref[...] = _layer_norm(alpha * h_ref[...] + mix, g_ref[...], b_ref[...])


def _out_proj(o_groups, w, h, g, b, alpha, tm=512):
    t, d = h.shape
    grp = pl.BlockSpec((tm, GROUP_WIDTH), lambda i: (i, 0))
    vec = pl.BlockSpec((1, d), lambda i: (0, 0))
    row = pl.BlockSpec((tm, d), lambda i: (i, 0))
    return pl.pallas_call(
        functools.partial(_out_proj_kernel, alpha=alpha),
        out_shape=jax.ShapeDtypeStruct((t, d), F32),
        grid=(t // tm,),
        in_specs=[grp, grp, grp, grp, pl.BlockSpec((d, d), lambda i: (0, 0)), row, vec, vec],
        out_specs=row,
        compiler_params=_params("parallel"),
        name="out_proj_ln",
    )(*o_groups, w, h, g, b)


def _ffn_kernel(e_ref, x_ref, w1_ref, w3_ref, w2_ref, o_ref, acc_ref):
    j = pl.program_id(1)
    x = x_ref[...]
    a = _dot(x, w1_ref[0])
    gate = a * (1.0 / (1.0 + jnp.exp(-a)))
    hid = (gate * _dot(x, w3_ref[0])).astype(BF16)
    part = _dot(hid, w2_ref[0])

    @pl.when(j == 0)
    def _():
        acc_ref[...] = part

    @pl.when(j > 0)
    def _():
        acc_ref[...] += part

    @pl.when(j == pl.num_programs(1) - 1)
    def _():
        o_ref[...] = acc_ref[...]


def _ffn(x, tile_expert, w1, w3, w2, tm, tf=512):
    rows, d = x.shape
    f = w1.shape[-1]
    return pl.pallas_call(
        _ffn_kernel,
        out_shape=jax.ShapeDtypeStruct((rows, d), F32),
        grid_spec=pltpu.PrefetchScalarGridSpec(
            num_scalar_prefetch=1,
            grid=(rows // tm, f // tf),
            in_specs=[pl.BlockSpec((tm, d), lambda i, j, e: (i, 0)),
                      pl.BlockSpec((1, d, tf), lambda i, j, e: (e[i], 0, j)),
                      pl.BlockSpec((1, d, tf), lambda i, j, e: (e[i], 0, j)),
                      pl.BlockSpec((1, tf, d), lambda i, j, e: (e[i], j, 0))],
            out_specs=pl.BlockSpec((tm, d), lambda i, j, e: (i, 0)),
            scratch_shapes=[pltpu.VMEM((tm, d), F32)]),
        compiler_params=_params("parallel", "arbitrary"),
        name="swiglu_ffn",
    )(tile_expert, x, w1, w3, w2)


def _post_kernel(h_ref, f0_ref, f1_ref, gt_ref, p_ref, g_ref, b_ref, wg_ref, wp_ref, o_ref, *, alpha):
    gt = gt_ref[...]
    f = gt[:, 0:1] * f0_ref[...] + gt[:, 1:2] * f1_ref[...]
    h2 = _layer_norm(alpha * h_ref[...] + f, g_ref[...], b_ref[...])
    gate = 1.0 / (1.0 + jnp.exp(-_dot(h2.astype(BF16), wg_ref[...])))
    o_ref[...] = h2 + gate * _dot(p_ref[...].astype(BF16), wp_ref[...])


def _post(h, f0, f1, gates, p, g, b, wg, wp, alpha, tm=512):
    t, d = h.shape
    pd = p.shape[1]
    row = pl.BlockSpec((tm, d), lambda i: (i, 0))
    vec = pl.BlockSpec((1, d), lambda i: (0, 0))
    return pl.pallas_call(
        functools.partial(_post_kernel, alpha=alpha),
        out_shape=jax.ShapeDtypeStruct((t, d), F32),
        grid=(t // tm,),
        in_specs=[row, row, row, pl.BlockSpec((tm, LANES), lambda i: (i, 0)),
                  pl.BlockSpec((tm, pd), lambda i: (i, 0)), vec, vec,
                  pl.BlockSpec((d, d), lambda i: (0, 0)), pl.BlockSpec((pd, d), lambda i: (0, 0))],
        out_specs=row,
        compiler_params=_params("parallel"),
        name="ln_ple",
    )(h, f0, f1, gates, p, g, b, wg, wp)


def _router_kernel(h_ref, w_ref, gate_ref, idx_ref):
    logits = _dot(h_ref[...], w_ref[...], HIGHEST)
    lane = lax.broadcasted_iota(jnp.int32, logits.shape, 1)
    logits = jnp.where(lane < N_EXPERTS, logits, -jnp.inf)
    m1 = jnp.max(logits, axis=-1, keepdims=True)
    i1 = jnp.min(jnp.where(logits == m1, lane, LANES), axis=-1, keepdims=True)
    rest = jnp.where(lane == i1, -jnp.inf, logits)
    m2 = jnp.max(rest, axis=-1, keepdims=True)
    i2 = jnp.min(jnp.where(rest == m2, lane, LANES), axis=-1, keepdims=True)
    e2 = jnp.exp(m2 - m1)
    den = 1.0 + e2
    gate_ref[...] = jnp.where(lane == 0, 1.0 / den, jnp.where(lane == 1, e2 / den, 0.0))
    idx_ref[...] = jnp.where(lane == 0, i1, jnp.where(lane == 1, i2, 0))


def _router(h, w_pad, tm=512):
    t, d = h.shape
    return pl.pallas_call(
        _router_kernel,
        out_shape=(jax.ShapeDtypeStruct((t, LANES), F32), jax.ShapeDtypeStruct((t, LANES), jnp.int32)),
        grid=(t // tm,),
        in_specs=[pl.BlockSpec((tm, d), lambda i: (i, 0)), pl.BlockSpec((d, LANES), lambda i: (0, 0))],
        out_specs=(pl.BlockSpec((tm, LANES), lambda i: (i, 0)), pl.BlockSpec((tm, LANES), lambda i: (i, 0))),
        compiler_params=_params("parallel"),
        name="moe_router",
    )(h, w_pad)


def _moe_dispatch(top_e, pad):
    t = top_e.shape[0]
    flat_e = top_e.reshape(-1)
    n_slots = flat_e.shape[0]
    order = jnp.argsort(flat_e)
    e_sorted = flat_e[order]
    counts = jnp.bincount(flat_e, length=N_EXPERTS)
    padded = (counts + pad - 1) // pad * pad
    start = jnp.cumsum(counts) - counts
    pad_end = jnp.cumsum(padded)
    pad_start = pad_end - padded
    dest = pad_start[e_sorted] + jnp.arange(n_slots) - start[e_sorted]
    n_rows = n_slots + N_EXPERTS * pad
    row_tok = jnp.zeros((n_rows,), jnp.int32).at[dest].set((order // 2).astype(jnp.int32))
    slot_row = jnp.zeros((n_slots,), jnp.int32).at[order].set(dest.astype(jnp.int32)).reshape(t, 2)
    tile_e = jnp.minimum(jnp.searchsorted(pad_end, jnp.arange(n_rows // pad) * pad, side='right'),
                         N_EXPERTS - 1).astype(jnp.int32)
    return row_tok, slot_row, tile_e


def _token_mixer(h, b, s, w_in_pad, tables, diff_lambda, diff_gain, cmp_pe, cmp_w1, cmp_w2, lam_init):
    t, d = h.shape
    proj = _matmul(h, w_in_pad, tm=min(1024, t), tn=512)
    rp, means = _rope(proj, *tables)
    proj = proj.reshape(b, s, -1)
    rp = rp.reshape(b, s, -1)
    tk = min(ATT_TILE, s)

    def cols(x, blk, n=2):
        return x[:, :, LANES * blk:LANES * (blk + n)]

    o_a = _stick_breaking(proj, BLK_AQ, _kt_tiles(cols(proj, BLK_AK), tk), cols(proj, BLK_AV).astype(BF16))
    gain2 = jnp.tile(diff_gain, 2).reshape(1, LANES)
    o_b = _diff_attention(rp, RP_BQ, _kt_tiles(cols(rp, RP_BK), tk), cols(proj, BLK_BV).astype(BF16),
                          diff_lambda, gain2, lam_init)
    nb = s // MOBA_BLOCK
    kmean = means.reshape(b, nb, -1)[:, :, LANES * RP_CK:LANES * (RP_CK + 2)]
    kmean = jnp.pad(kmean, ((0, 0), (0, LANES - nb), (0, 0)))
    o_c = _moba(rp, kmean, _kt_tiles(cols(rp, RP_CK), tk), cols(proj, BLK_CV).astype(BF16))
    n_chunks = s // CMP_STRIDE
    kvc = cols(proj, BLK_KVC, 1)
    chunks = jnp.stack([kvc[:, :, :HEAD_DIM], kvc[:, :, HEAD_DIM:]]).reshape(2, b, n_chunks, CMP_STRIDE * HEAD_DIM)
    cmp = _compress(chunks, cmp_pe.reshape(2, 2, CMP_STRIDE * HEAD_DIM),
                    cmp_w1.reshape(2, 2, CMP_STRIDE * HEAD_DIM, -1), cmp_w2)
    cmp_id = jnp.arange(n_chunks)[:, None]
    sel_id = jnp.arange(LANES)[None, :]
    overlap = ((cmp_id * CMP_STRIDE < (sel_id + 1) * SEL_BLOCK)
               & (cmp_id * CMP_STRIDE + CMP_LEN > sel_id * SEL_BLOCK)).astype(BF16)
    o_cmp, selmask = _nsa_cmp(proj, cmp[0], jnp.tile(cmp[1], (1, 1, N_HEADS)), overlap)
    kvs = cols(rp, RP_KVS, 1)
    kvw = cols(rp, RP_KVW, 1)

    def key_t(kv):
        return _kt_tiles(jnp.pad(kv[:, :, :HEAD_DIM], ((0, 0), (0, 0), (0, HEAD_DIM))), tk)[:, 0, :, :HEAD_DIM]

    def val2(kv):
        v = kv[:, :, HEAD_DIM:].astype(BF16)
        return jnp.concatenate([v, v], axis=-1)

    o_d = _nsa_main(rp, proj, key_t(kvs), val2(kvs), key_t(kvw), val2(kvw), selmask, o_cmp)
    return [o.reshape(t, GROUP_WIDTH) for o in (o_a, o_b, o_c, o_d)]


def kernel(x, p, positions, w_in, w_out, ln_mix_g, ln_mix_b, diff_lambda, diff_gain, nsa_cmp_pos, nsa_cmp_w1, nsa_cmp_w2, ffn_w1, ffn_w3, ffn_w2, moe_router, moe_w1, moe_w3, moe_w2, ln_ffn_g, ln_ffn_b, ple_proj, ple_gate):
    b, s, d = x.shape
    t = b * s
    depth = w_in.shape[0]
    alpha = (2 * depth) ** 0.25
    tables = _rope_tables(positions)
    h = x.reshape(t, d)
    ones_gate = jnp.zeros((t, LANES), F32).at[:, 0].set(1.0)
    for i in range(depth):
        w_in_pad = jnp.pad(w_in[i], ((0, 0), (0, N_IN_PAD - w_in.shape[2]))).astype(BF16)
        lam_init = 0.8 - 0.6 * math.exp(-0.3 * i)
        groups = _token_mixer(h, b, s, w_in_pad, tables, diff_lambda[i], diff_gain[i],
                              nsa_cmp_pos[i], nsa_cmp_w1[i], nsa_cmp_w2[i], lam_init)
        h = _out_proj(groups, w_out[i].astype(BF16), h, ln_mix_g[i:i + 1], ln_mix_b[i:i + 1], alpha)
        if i % 2 == 0:
            j = i // 2
            tm = min(1024, t)
            f0 = _ffn(h.astype(BF16), jnp.zeros((t // tm,), jnp.int32), ffn_w1[j:j + 1].astype(BF16),
                      ffn_w3[j:j + 1].astype(BF16), ffn_w2[j:j + 1].astype(BF16), tm)
            f1, gates = f0, ones_gate
        else:
            j = i // 2
            wr = jnp.pad(moe_router[j], ((0, 0), (0, LANES - N_EXPERTS)))
            gates, idx = _router(h, wr)
            row_tok, slot_row, tile_e = _moe_dispatch(idx[:, :2], MOE_ROWS)
            xg = h.astype(BF16)[row_tok]
            y = _ffn(xg, tile_e, moe_w1[j].astype(BF16), moe_w3[j].astype(BF16), moe_w2[j].astype(BF16), MOE_ROWS)
            f0, f1 = y[slot_row[:, 0]], y[slot_row[:, 1]]
        h = _post(h, f0, f1, gates, p[i].reshape(t, -1), ln_ffn_g[i:i + 1], ln_ffn_b[i:i + 1],
                  ple_gate[i].astype(BF16), ple_proj[i].astype(BF16), alpha)
    return h.reshape(b, s, d)
```

```python
import functools
import math

import jax
import jax.numpy as jnp
from jax import lax
from jax.experimental import pallas as pl
from jax.experimental.pallas import tpu as pltpu

F32 = jnp.float32
BF16 = jnp.bfloat16
HIGHEST = lax.Precision.HIGHEST

LANES = 128
HEAD_DIM = 64
GROUP_WIDTH = 256
N_HEADS = 4
DIFF_HALF = 32
ROPE_THETA = 10000.0
MOBA_BLOCK = 256
MOBA_TOPK = 3
CMP_LEN = 32
CMP_STRIDE = 16
SEL_BLOCK = 64
SEL_TOPK = 16
WINDOW = 512
N_EXPERTS = 8
LN_EPS = 1e-5
RMS_EPS = 1e-5
NEG = -1e30
LOG2E = 1.4426950408889634
ATT_TILE = 512
SUB_TILE = 256
MOE_ROWS = 512
VMEM_LIMIT = 48 * 1024 * 1024

BLK_AQ, BLK_AK, BLK_AV = 0, 2, 4
BLK_BQ, BLK_BK, BLK_BV = 6, 8, 10
BLK_CQ, BLK_CK, BLK_CV = 12, 14, 16
BLK_DQ, BLK_KVC, BLK_KVS, BLK_KVW, BLK_G = 18, 20, 21, 22, 23
N_IN_PAD = 24 * LANES
ROPE_SRC = (BLK_BQ, BLK_BQ + 1, BLK_BK, BLK_BK + 1,
            BLK_CQ, BLK_CQ + 1, BLK_CK, BLK_CK + 1, BLK_DQ, BLK_DQ + 1, BLK_KVS, BLK_KVW)
ROPE_KIND = (1, 1, 1, 1, 0, 0, 0, 0, 0, 0, 2, 2)
RP_BQ, RP_BK, RP_CQ, RP_CK, RP_DQ, RP_KVS, RP_KVW = 0, 2, 4, 6, 8, 10, 11


def _params(*sem):
    return pltpu.CompilerParams(dimension_semantics=sem, vmem_limit_bytes=VMEM_LIMIT)


def _dot(a, b, precision=None):
    return jnp.dot(a, b, preferred_element_type=F32, precision=precision)


def _dot_split_rhs(a_exact, b, pieces):
    out = None
    rest = b
    for _ in range(pieces):
        part = rest.astype(BF16)
        rest = rest - part.astype(F32)
        term = _dot(a_exact, part)
        out = term if out is None else out + term
    return out


def _layer_norm(x, g, b):
    mu = jnp.mean(x, axis=-1, keepdims=True)
    xc = x - mu
    var = jnp.mean(xc * xc, axis=-1, keepdims=True)
    return xc * lax.rsqrt(var + LN_EPS) * g + b


def _matmul_kernel(x_ref, w_ref, o_ref, xb_ref):
    @pl.when(pl.program_id(1) == 0)
    def _():
        xb_ref[...] = x_ref[...].astype(BF16)

    o_ref[...] = _dot(xb_ref[...], w_ref[...])


def _matmul(x, w, tm, tn):
    m, k = x.shape
    n = w.shape[1]
    return pl.pallas_call(
        _matmul_kernel,
        out_shape=jax.ShapeDtypeStruct((m, n), F32),
        grid=(m // tm, n // tn),
        in_specs=[pl.BlockSpec((tm, k), lambda i, j: (i, 0)),
                  pl.BlockSpec((k, tn), lambda i, j: (0, j))],
        out_specs=pl.BlockSpec((tm, tn), lambda i, j: (i, j)),
        scratch_shapes=[pltpu.VMEM((tm, k), BF16)],
        compiler_params=_params("parallel", "arbitrary"),
        name="in_proj",
    )(x, w)


def _rope_kernel(x_ref, c_ref, s_ref, o_ref, mean_ref):
    tm = x_ref.shape[0]
    lane = lax.broadcasted_iota(jnp.int32, (tm, LANES), 1)
    for j, (src, kind) in enumerate(zip(ROPE_SRC, ROPE_KIND)):
        x = x_ref[:, LANES * src:LANES * (src + 1)]
        half = 16 if kind == 1 else 32
        swap = jnp.where((lane & half) == 0, pltpu.roll(x, LANES - half, 1), pltpu.roll(x, half, 1))
        y = x * c_ref[kind] + swap * s_ref[kind]
        o_ref[:, LANES * j:LANES * (j + 1)] = y
        if RP_CK <= j < RP_CK + 2:
            mean_ref[0, :, LANES * (j - RP_CK):LANES * (j - RP_CK + 1)] = jnp.mean(y, axis=0, keepdims=True)


def _rope(proj, cos_t, sin_t):
    t, n_in = proj.shape
    tm = MOBA_BLOCK
    n_out = len(ROPE_SRC)
    tab_spec = pl.BlockSpec((3, tm, LANES), lambda i: (0, i, 0))
    return pl.pallas_call(
        _rope_kernel,
        out_shape=(jax.ShapeDtypeStruct((t, n_out * LANES), F32),
                   jax.ShapeDtypeStruct((t // tm, 1, GROUP_WIDTH), F32)),
        grid=(t // tm,),
        in_specs=[pl.BlockSpec((tm, n_in), lambda i: (i, 0)), tab_spec, tab_spec],
        out_specs=(pl.BlockSpec((tm, n_out * LANES), lambda i: (i, 0)),
                   pl.BlockSpec((1, 1, GROUP_WIDTH), lambda i: (i, 0, 0))),
        compiler_params=_params("parallel"),
        name="rope",
    )(proj, cos_t, sin_t)


def _rope_tables(positions):
    pos = positions.astype(F32).reshape(-1, 1)
    lane = jnp.arange(LANES)

    def tables(half, active):
        inv_freq = ROPE_THETA ** (-jnp.arange(half, dtype=F32) / half)
        ang = pos * inv_freq[lane % half][None, :]
        sign = jnp.where((lane & half) == 0, -1.0, 1.0)
        cos = jnp.where(active, jnp.cos(ang), 1.0)
        sin = jnp.where(active, jnp.sin(ang) * sign, 0.0)
        return cos, sin

    c0, s0 = tables(32, lane >= 0)
    c1, s1 = tables(16, lane >= 0)
    c2, s2 = tables(32, lane < HEAD_DIM)
    return jnp.stack([c0, c1, c2]), jnp.stack([s0, s1, s2])


def _rows(x, lo, n):
    r = lax.broadcasted_iota(jnp.int32, x.shape, 0)
    return jnp.where((r >= lo) & (r < lo + n), x, 0.0)


def _softmax_step_t(st, m, l, acc, vt):
    m_new = jnp.maximum(m, jnp.max(st, axis=0, keepdims=True))
    alpha = jnp.exp2(m - m_new)
    p = jnp.exp2(st - m_new)
    l = alpha * l + jnp.sum(p, axis=0, keepdims=True)
    acc = alpha * acc + _dot(vt, p.astype(BF16))
    return m_new, l, acc


def _softmax_init(n, tq):
    return tuple((jnp.full((1, tq), NEG, F32), jnp.zeros((1, tq), F32), jnp.zeros((HEAD_DIM, tq), F32))
                 for _ in range(n))


def _descending_tiles(tile, qi, init):
    carry = tile(qi, init, True)
    return lax.fori_loop(0, qi, lambda s, c: tile(qi - 1 - s, c, False), carry)


def _vt_tiles(v, tk):
    b, s, c = v.shape
    v = v.astype(BF16).reshape(b, s // tk, tk, c // LANES, LANES)
    return v.transpose(0, 3, 1, 4, 2)


def _pair_specs(s, tq, nt, q_blk):
    return [pl.BlockSpec((1, tq, LANES), lambda bi, hp, qi: (bi, qi, q_blk + hp)),
            pl.BlockSpec((1, s, LANES), lambda bi, hp, qi: (bi, 0, hp)),
            pl.BlockSpec((1, 1, nt, LANES, tq), lambda bi, hp, qi: (bi, hp, 0, 0, 0))]


def _sb_kernel(q_ref, k_ref, vt_ref, tri_ref, o_ref):
    tq = q_ref.shape[1]
    sub = tri_ref.shape[0]
    qi = pl.program_id(2)
    qt = (q_ref[0] * HEAD_DIM ** -0.5).T
    qh = [_rows(qt, HEAD_DIM * h, HEAD_DIM).astype(BF16) for h in range(2)]
    tri = tri_ref[...]
    key = lax.broadcasted_iota(jnp.int32, (sub, tq), 0)
    qry = lax.broadcasted_iota(jnp.int32, (sub, tq), 1)

    def tile(t, carry, diag):
        vt = vt_ref[0, 0, t]
        carry = list(carry)
        for part in reversed(range(tq // sub)):
            kt = k_ref[0, pl.ds(pl.multiple_of(t * tq + part * sub, sub), sub), :]
            past = key + part * sub < qry
            zs = [_dot(kt, qh[h]) for h in range(2)]
            for h in range(2):
                acc, run = carry[h]
                z = zs[h]
                lg = -(jnp.maximum(z, 0.0) + jnp.log(1.0 + jnp.exp(-jnp.abs(z))))
                if diag:
                    lg = jnp.where(past, lg, 0.0)
                cum = _dot_split_rhs(tri, lg, 2)
                w = jnp.exp(z + cum + run)
                if diag:
                    w = jnp.where(past, w, 0.0)
                pv = _dot(vt[HEAD_DIM * h:HEAD_DIM * (h + 1), part * sub:(part + 1) * sub], w.astype(BF16))
                carry[h] = (acc + pv, run + cum[0:1])
        return tuple(carry)

    init = tuple((jnp.zeros((HEAD_DIM, tq), F32), jnp.zeros((1, tq), F32)) for _ in range(2))
    carry = _descending_tiles(tile, qi, init)
    o_ref[0] = jnp.concatenate([carry[0][0], carry[1][0]], axis=0).T.astype(o_ref.dtype)


def _stick_breaking(q_src, q_blk, k, vt):
    b, s, _ = q_src.shape
    tq = min(ATT_TILE, s)
    nt = s // tq
    sub = min(SUB_TILE, tq)
    tri = (jnp.arange(sub)[None, :] >= jnp.arange(sub)[:, None]).astype(BF16)
    return pl.pallas_call(
        _sb_kernel,
        out_shape=jax.ShapeDtypeStruct((b, s, GROUP_WIDTH), BF16),
        grid=(b, 2, nt),
        in_specs=_pair_specs(s, tq, nt, q_blk) + [pl.BlockSpec((sub, sub), lambda bi, hp, qi: (0, 0))],
        out_specs=pl.BlockSpec((1, tq, LANES), lambda bi, hp, qi: (bi, qi, hp)),
        compiler_params=_params("parallel", "parallel", "arbitrary"),
        name="stick_breaking",
    )(q_src, k, vt, tri)


def _diff_kernel(q_ref, k_ref, vt_ref, lam_ref, gain_ref, o_ref, *, lam_init):
    tq = q_ref.shape[1]
    qi = pl.program_id(2)
    qt = (q_ref[0] * (DIFF_HALF ** -0.5 * LOG2E)).T
    qc = [_rows(qt, DIFF_HALF * c, DIFF_HALF).astype(BF16) for c in range(4)]
    key = lax.broadcasted_iota(jnp.int32, (tq, tq), 0)
    qry = lax.broadcasted_iota(jnp.int32, (tq, tq), 1)
    causal = key <= qry

    def tile(t, carry, diag):
        kt = k_ref[0, pl.ds(pl.multiple_of(t * tq, tq), tq), :]
        vt = vt_ref[0, 0, t]
        scores = [_dot(kt, qc[c]) for c in range(4)]
        new = []
        for c in range(4):
            s = jnp.where(causal, scores[c], NEG) if diag else scores[c]
            h = c // 2
            new.append(_softmax_step_t(s, *carry[c], vt[HEAD_DIM * h:HEAD_DIM * (h + 1)]))
        return tuple(new)

    st = _descending_tiles(tile, qi, _softmax_init(4, tq))
    lf = lam_ref[...]
    lam = (jnp.exp(jnp.sum(lf[0:1] * lf[1:2], axis=-1, keepdims=True))
           - jnp.exp(jnp.sum(lf[2:3] * lf[3:4], axis=-1, keepdims=True)) + lam_init)
    norm = [st[c][2] * (1.0 / st[c][1]) for c in range(4)]
    o = jnp.concatenate([norm[0] - lam * norm[1], norm[2] - lam * norm[3]], axis=0).T
    lane = lax.broadcasted_iota(jnp.int32, (tq, LANES), 1)
    first = lane < HEAD_DIM
    sq = o * o
    ms = jnp.where(first, jnp.sum(jnp.where(first, sq, 0.0), axis=-1, keepdims=True),
                   jnp.sum(jnp.where(first, 0.0, sq), axis=-1, keepdims=True)) * (1.0 / HEAD_DIM)
    o_ref[0] = (o * lax.rsqrt(ms + RMS_EPS) * gain_ref[...] * (1.0 - lam_init)).astype(o_ref.dtype)


def _diff_attention(q_src, q_blk, k, vt, lam_params, gain2, lam_init):
    b, s, _ = q_src.shape
    tq = min(ATT_TILE, s)
    nt = s // tq
    return pl.pallas_call(
        functools.partial(_diff_kernel, lam_init=lam_init),
        out_shape=jax.ShapeDtypeStruct((b, s, GROUP_WIDTH), BF16),
        grid=(b, 2, nt),
        in_specs=_pair_specs(s, tq, nt, q_blk) + [pl.BlockSpec((4, DIFF_HALF), lambda bi, hp, qi: (0, 0)),
                                                   pl.BlockSpec((1, LANES), lambda bi, hp, qi: (0, 0))],
        out_specs=pl.BlockSpec((1, tq, LANES), lambda bi, hp, qi: (bi, qi, hp)),
        compiler_params=_params("parallel", "parallel", "arbitrary"),
        name="diff_attention",
    )(q_src, k, vt, lam_params, gain2)


def _topk_mask_t(work, k):
    n = work.shape[0]
    row = lax.broadcasted_iota(jnp.int32, work.shape, 0)
    sel = jnp.zeros(work.shape, jnp.bool_)
    for _ in range(k):
        mx = jnp.max(work, axis=0, keepdims=True)
        idx = jnp.min(jnp.where(work == mx, row, n), axis=0, keepdims=True)
        pick = row == idx
        sel = sel | pick
        work = jnp.where(pick, -jnp.inf, work)
    return sel


def _moba_kernel(q_ref, k_ref, vt_ref, kmean_ref, o_ref, allow_ref):
    tq = q_ref.shape[1]
    per_tile = tq // MOBA_BLOCK
    qi = pl.program_id(2)
    qt = q_ref[0].T
    qs = qt * (HEAD_DIM ** -0.5 * LOG2E)
    qh = [_rows(qs, HEAD_DIM * h, HEAD_DIM).astype(BF16) for h in range(2)]
    km = kmean_ref[0]
    lane = lax.broadcasted_iota(jnp.int32, km.shape, 1)
    blk = lax.broadcasted_iota(jnp.int32, (LANES, tq), 0)
    own = qi * per_tile + lax.broadcasted_iota(jnp.int32, (1, tq), 1) // MOBA_BLOCK
    past_blk = blk < own
    for h in range(2):
        km_h = jnp.where((lane >= HEAD_DIM * h) & (lane < HEAD_DIM * (h + 1)), km, 0.0)
        gate = _dot(km_h, qt, HIGHEST)
        allowed = _topk_mask_t(jnp.where(past_blk, gate, -1e38), MOBA_TOPK) & past_blk
        allow_ref[h] = jnp.where(allowed, 1.0, 0.0)
    key = lax.broadcasted_iota(jnp.int32, (tq, tq), 0)
    qry = lax.broadcasted_iota(jnp.int32, (tq, tq), 1)
    same_blk = key // MOBA_BLOCK == qry // MOBA_BLOCK

    def block_mask(h, t):
        rows = [jnp.broadcast_to(allow_ref[h, pl.ds(t * per_tile + i, 1), :], (MOBA_BLOCK, tq))
                for i in range(per_tile)]
        return jnp.concatenate(rows, axis=0) > 0.5

    def tile(t, carry, diag):
        kt = k_ref[0, pl.ds(pl.multiple_of(t * tq, tq), tq), :]
        vt = vt_ref[0, 0, t]
        scores = [_dot(kt, qh[h]) for h in range(2)]
        new = []
        for h in range(2):
            ok = block_mask(h, t)
            if diag:
                ok = (key <= qry) & (same_blk | ok)
            new.append(_softmax_step_t(jnp.where(ok, scores[h], NEG), *carry[h],
                                       vt[HEAD_DIM * h:HEAD_DIM * (h + 1)]))
        return tuple(new)

    st = _descending_tiles(tile, qi, _softmax_init(2, tq))
    o = jnp.concatenate([st[h][2] * (1.0 / st[h][1]) for h in range(2)], axis=0)
    o_ref[0] = o.T.astype(o_ref.dtype)


def _moba(rp, k, vt, kmean):
    b, s, _ = rp.shape
    tq = min(ATT_TILE, s)
    nt = s // tq
    return pl.pallas_call(
        _moba_kernel,
        out_shape=jax.ShapeDtypeStruct((b, s, GROUP_WIDTH), BF16),
        grid=(b, 2, nt),
        in_specs=_pair_specs(s, tq, nt, RP_CQ) + [pl.BlockSpec((1, LANES, LANES), lambda bi, hp, qi: (bi, 0, hp))],
        out_specs=pl.BlockSpec((1, tq, LANES), lambda bi, hp, qi: (bi, qi, hp)),
        scratch_shapes=[pltpu.VMEM((2, LANES, tq), F32)],
        compiler_params=_params("parallel", "parallel", "arbitrary"),
        name="moba",
    )(rp, k, vt, kmean)


def _compress_kernel(c_ref, pe_ref, w1_ref, w2_ref, o_ref):
    c = c_ref[0, 0]
    n = c.shape[0]
    y0 = _dot(c + pe_ref[0, 0:1], w1_ref[0, 0], HIGHEST)
    y1 = _dot(c + pe_ref[0, 1:2], w1_ref[0, 1], HIGHEST)
    pre = y0 + pltpu.roll(y1, n - 1, 0)
    hid = pre * (1.0 / (1.0 + jnp.exp(-pre)))
    o_ref[0, 0] = _dot(hid, w2_ref[0], HIGHEST)


def _compress(chunks, pe, w1, w2):
    _, b, n, width = chunks.shape
    hidden = w1.shape[-1]
    return pl.pallas_call(
        _compress_kernel,
        out_shape=jax.ShapeDtypeStruct((2, b, n, HEAD_DIM), F32),
        grid=(2, b),
        in_specs=[pl.BlockSpec((1, 1, n, width), lambda j, bi: (j, bi, 0, 0)),
                  pl.BlockSpec((1, 2, width), lambda j, bi: (j, 0, 0)),
                  pl.BlockSpec((1, 2, width, hidden), lambda j, bi: (j, 0, 0, 0)),
                  pl.BlockSpec((1, hidden, HEAD_DIM), lambda j, bi: (j, 0, 0))],
        out_specs=pl.BlockSpec((1, 1, n, HEAD_DIM), lambda j, bi: (j, bi, 0, 0)),
        compiler_params=_params("parallel", "parallel"),
        name="nsa_compress",
    )(chunks, pe, w1, w2)


def _nsa_cmp_kernel(q_ref, kc_ref, vct_ref, ovt_ref, oc_ref, sel_ref):
    tq = q_ref.shape[1]
    n_cmp = kc_ref.shape[1]
    qi = pl.program_id(1)
    q = q_ref[0]
    qts = [q[:, :LANES].T, q[:, LANES:].T]
    kc = kc_ref[0]
    vct = vct_ref[0].astype(BF16)
    qpos = qi * tq + lax.broadcasted_iota(jnp.int32, (1, tq), 1)
    cmp_end = lax.broadcasted_iota(jnp.int32, (n_cmp, 1), 0) * CMP_STRIDE + (CMP_LEN - 1)
    valid = cmp_end <= qpos
    p_sum = jnp.zeros((n_cmp, tq), F32)
    outs = []
    for h in range(N_HEADS):
        lo = HEAD_DIM * (h % 2)
        s = _dot(kc, qts[h // 2][lo:lo + HEAD_DIM], HIGHEST) * HEAD_DIM ** -0.5
        m = jnp.max(jnp.where(valid, s, NEG), axis=0, keepdims=True)
        e = jnp.where(valid, jnp.exp(s - m), 0.0)
        den = jnp.sum(e, axis=0, keepdims=True)
        p = e / jnp.where(den > 0.0, den, 1.0)
        p_sum = p_sum + p
        outs.append(_dot(vct, p.astype(BF16)))
    oc_ref[0] = jnp.concatenate(outs, axis=0)
    imp = _dot_split_rhs(ovt_ref[...], p_sum, 3)
    blk = lax.broadcasted_iota(jnp.int32, (LANES, tq), 0)
    qblk = qpos // SEL_BLOCK
    forced = (blk == 0) | (blk == qblk) | (blk == qblk - 1)
    work = jnp.where(forced, 1e30, jnp.where(blk <= qblk, imp, -1.0))
    sel = _topk_mask_t(work, SEL_TOPK) & (blk <= qblk)
    sel_ref[0] = jnp.where(sel, 1.0, 0.0)


def _nsa_cmp(proj, kc, vct, overlap_t):
    b, s, _ = proj.shape
    tq = min(SUB_TILE, s)
    n_cmp = kc.shape[1]
    return pl.pallas_call(
        _nsa_cmp_kernel,
        out_shape=(jax.ShapeDtypeStruct((b, GROUP_WIDTH, s), F32),
                   jax.ShapeDtypeStruct((b, LANES, s), F32)),
        grid=(b, s // tq),
        in_specs=[pl.BlockSpec((1, tq, GROUP_WIDTH), lambda bi, qi: (bi, qi, BLK_DQ // 2)),
                  pl.BlockSpec((1, n_cmp, HEAD_DIM), lambda bi, qi: (bi, 0, 0)),
                  pl.BlockSpec((1, HEAD_DIM, n_cmp), lambda bi, qi: (bi, 0, 0)),
                  pl.BlockSpec((LANES, n_cmp), lambda bi, qi: (0, 0))],
        out_specs=(pl.BlockSpec((1, GROUP_WIDTH, tq), lambda bi, qi: (bi, 0, qi)),
                   pl.BlockSpec((1, LANES, tq), lambda bi, qi: (bi, 0, qi))),
        compiler_params=_params("parallel", "arbitrary"),
        name="nsa_compressed",
    )(proj, kc, vct, overlap_t)


def _nsa_main_kernel(q_ref, ks_ref, vst_ref, kw_ref, vwt_ref, sel_ref, oc_ref, g_ref, o_ref):
    tq = q_ref.shape[1]
    qi = pl.program_id(1)
    q = q_ref[0] * (HEAD_DIM ** -0.5 * LOG2E)
    qts = [q[:, :LANES].T, q[:, LANES:].T]
    zero = jnp.zeros((HEAD_DIM, tq), F32)
    qh = [jnp.concatenate([qts[h // 2][HEAD_DIM * (h % 2):HEAD_DIM * (h % 2 + 1)], zero], axis=0).astype(BF16)
          for h in range(N_HEADS)]
    key = lax.broadcasted_iota(jnp.int32, (tq, tq), 0)
    qry = lax.broadcasted_iota(jnp.int32, (tq, tq), 1)
    per_tile = tq // SEL_BLOCK

    def sel_tile(t, carry, diag):
        rows = sel_ref[0, pl.ds(pl.multiple_of(t * per_tile, per_tile), per_tile), :]
        ok = jnp.concatenate([jnp.broadcast_to(rows[i:i + 1], (SEL_BLOCK, tq)) for i in range(per_tile)],
                             axis=0) > 0.5
        if diag:
            ok = ok & (key <= qry)
        kt = ks_ref[0, pl.ds(pl.multiple_of(t * tq, tq), tq), :]
        vt = vst_ref[0, t]
        scores = [_dot(kt, qh[h]) for h in range(N_HEADS)]
        return tuple(_softmax_step_t(jnp.where(ok, scores[h], NEG), *carry[h], vt) for h in range(N_HEADS))

    st = _descending_tiles(sel_tile, qi, _softmax_init(N_HEADS, tq))

    def win_tile(t, carry):
        delta = (qi - t) * tq + qry - key
        ok = (delta >= 0) & (delta < WINDOW)
        kt = kw_ref[0, pl.ds(pl.multiple_of(t * tq, tq), tq), :]
        vt = vwt_ref[0, t]
        scores = [_dot(kt, qh[h]) for h in range(N_HEADS)]
        return tuple(_softmax_step_t(jnp.where(ok, scores[h], NEG), *carry[h], vt) for h in range(N_HEADS))

    n_back = (WINDOW + tq - 1) // tq
    wt = win_tile(qi, _softmax_init(N_HEADS, tq))
    wt = lax.fori_loop(0, jnp.minimum(qi, n_back), lambda s_, cr: win_tile(qi - 1 - s_, cr), wt)

    g = 1.0 / (1.0 + jnp.exp(-g_ref[0].T))
    outs = []
    for h in range(N_HEADS):
        o_c = oc_ref[0, HEAD_DIM * h:HEAD_DIM * (h + 1), :]
        o_s = st[h][2] * (1.0 / st[h][1])
        o_w = wt[h][2] * (1.0 / wt[h][1])
        outs.append(g[3 * h:3 * h + 1] * o_c + g[3 * h + 1:3 * h + 2] * o_s + g[3 * h + 2:3 * h + 3] * o_w)
    o_ref[0, :, :LANES] = jnp.concatenate(outs[:2], axis=0).T.astype(o_ref.dtype)
    o_ref[0, :, LANES:] = jnp.concatenate(outs[2:], axis=0).T.astype(o_ref.dtype)


def _nsa_main(rp, proj, ks, vst, kw, vwt, sel_t, o_cmp_t):
    b, s, _ = rp.shape
    tq = min(ATT_TILE, s)
    nt = s // tq
    kv_spec = pl.BlockSpec((1, s, LANES), lambda bi, qi: (bi, 0, 0))
    vt_spec = pl.BlockSpec((1, nt, HEAD_DIM, tq), lambda bi, qi: (bi, 0, 0, 0))
    return pl.pallas_call(
        _nsa_main_kernel,
        out_shape=jax.ShapeDtypeStruct((b, s, GROUP_WIDTH), BF16),
        grid=(b, nt),
        in_specs=[pl.BlockSpec((1, tq, GROUP_WIDTH), lambda bi, qi: (bi, qi, RP_DQ // 2)),
                  kv_spec, vt_spec, kv_spec, vt_spec,
                  pl.BlockSpec((1, LANES, tq), lambda bi, qi: (bi, 0, qi)),
                  pl.BlockSpec((1, GROUP_WIDTH, tq), lambda bi, qi: (bi, 0, qi)),
                  pl.BlockSpec((1, tq, LANES), lambda bi, qi: (bi, qi, BLK_G))],
        out_specs=pl.BlockSpec((1, tq, GROUP_WIDTH), lambda bi, qi: (bi, qi, 0)),
        compiler_params=_params("parallel", "arbitrary"),
        name="nsa_selected_window",
    )(rp, ks, vst, kw, vwt, sel_t, o_cmp_t, proj)


def _out_proj_kernel(oa_ref, ob_ref, oc_ref, od_ref, w_ref, h_ref, g_ref, b_ref, o_ref, *, alpha):
    mix = None
    for i, ref in enumerate((oa_ref, ob_ref, oc_ref, od_ref)):
        term = _dot(ref[...], w_ref[GROUP_WIDTH * i:GROUP_WIDTH * (i + 1), :])
        mix = term if mix is None else mix + term
    o_ref[...] = _layer_norm(alpha * h_ref[...] + mix, g_ref[...], b_ref[...])


def _out_proj(o_groups, w, h, g, b, alpha, tm=512):
    t, d = h.shape
    grp = pl.BlockSpec((tm, GROUP_WIDTH), lambda i: (i, 0))
    vec = pl.BlockSpec((1, d), lambda i: (0, 0))
    row = pl.BlockSpec((tm, d), lambda i: (i, 0))
    return pl.pallas_call(
        functools.partial(_out_proj_kernel, alpha=alpha),
        out_shape=jax.ShapeDtypeStruct((t, d), F32),
        grid=(t // tm,),
        in_specs=[grp, grp, grp, grp, pl.BlockSpec((d, d), lambda i: (0, 0)), row, vec, vec],
        out_specs=row,
        compiler_params=_params("parallel"),
        name="out_proj_ln",
    )(*o_groups, w, h, g, b)


def _ffn_kernel(e_ref, x_ref, w1_ref, w3_ref, w2_ref, o_ref, acc_ref):
    j = pl.program_id(1)
    x = x_ref[...]
    a = _dot(x, w1_ref[0])
    gate = a * (1.0 / (1.0 + jnp.exp(-a)))
    hid = (gate * _dot(x, w3_ref[0])).astype(BF16)
    part = _dot(hid, w2_ref[0])

    @pl.when(j == 0)
    def _():
        acc_ref[...] = part

    @pl.when(j > 0)
    def _():
        acc_ref[...] += part

    @pl.when(j == pl.num_programs(1) - 1)
    def _():
        o_ref[...] = acc_ref[...]


def _ffn(x, tile_expert, w1, w3, w2, tm, tf=512):
    rows, d = x.shape
    f = w1.shape[-1]
    return pl.pallas_call(
        _ffn_kernel,
        out_shape=jax.ShapeDtypeStruct((rows, d), F32),
        grid_spec=pltpu.PrefetchScalarGridSpec(
            num_scalar_prefetch=1,
            grid=(rows // tm, f // tf),
            in_specs=[pl.BlockSpec((tm, d), lambda i, j, e: (i, 0)),
                      pl.BlockSpec((1, d, tf), lambda i, j, e: (e[i], 0, j)),
                      pl.BlockSpec((1, d, tf), lambda i, j, e: (e[i], 0, j)),
                      pl.BlockSpec((1, tf, d), lambda i, j, e: (e[i], j, 0))],
            out_specs=pl.BlockSpec((tm, d), lambda i, j, e: (i, 0)),
            scratch_shapes=[pltpu.VMEM((tm, d), F32)]),
        compiler_params=_params("parallel", "arbitrary"),
        name="swiglu_ffn",
    )(tile_expert, x, w1, w3, w2)


def _post_kernel(h_ref, f0_ref, f1_ref, gt_ref, p_ref, g_ref, b_ref, wg_ref, wp_ref, o_ref, *, alpha):
    gt = gt_ref[...]
    f = gt[:, 0:1] * f0_ref[...] + gt[:, 1:2] * f1_ref[...]
    h2 = _layer_norm(alpha * h_ref[...] + f, g_ref[...], b_ref[...])
    gate = 1.0 / (1.0 + jnp.exp(-_dot(h2.astype(BF16), wg_ref[...])))
    o_ref[...] = h2 + gate * _dot(p_ref[...].astype(BF16), wp_ref[...])


def _post(h, f0, f1, gates, p, g, b, wg, wp, alpha, tm=512):
    t, d = h.shape
    pd = p.shape[1]
    row = pl.BlockSpec((tm, d), lambda i: (i, 0))
    vec = pl.BlockSpec((1, d), lambda i: (0, 0))
    return pl.pallas_call(
        functools.partial(_post_kernel, alpha=alpha),
        out_shape=jax.ShapeDtypeStruct((t, d), F32),
        grid=(t // tm,),
        in_specs=[row, row, row, pl.BlockSpec((tm, LANES), lambda i: (i, 0)),
                  pl.BlockSpec((tm, pd), lambda i: (i, 0)), vec, vec,
                  pl.BlockSpec((d, d), lambda i: (0, 0)), pl.BlockSpec((pd, d), lambda i: (0, 0))],
        out_specs=row,
        compiler_params=_params("parallel"),
        name="ln_ple",
    )(h, f0, f1, gates, p, g, b, wg, wp)


def _router_kernel(h_ref, w_ref, gate_ref, idx_ref):
    logits = _dot(h_ref[...], w_ref[...], HIGHEST)
    lane = lax.broadcasted_iota(jnp.int32, logits.shape, 1)
    logits = jnp.where(lane < N_EXPERTS, logits, -jnp.inf)
    m1 = jnp.max(logits, axis=-1, keepdims=True)
    i1 = jnp.min(jnp.where(logits == m1, lane, LANES), axis=-1, keepdims=True)
    rest = jnp.where(lane == i1, -jnp.inf, logits)
    m2 = jnp.max(rest, axis=-1, keepdims=True)
    i2 = jnp.min(jnp.where(rest == m2, lane, LANES), axis=-1, keepdims=True)
    e2 = jnp.exp(m2 - m1)
    den = 1.0 + e2
    gate_ref[...] = jnp.where(lane == 0, 1.0 / den, jnp.where(lane == 1, e2 / den, 0.0))
    idx_ref[...] = jnp.where(lane == 0, i1, jnp.where(lane == 1, i2, 0))


def _router(h, w_pad, tm=512):
    t, d = h.shape
    return pl.pallas_call(
        _router_kernel,
        out_shape=(jax.ShapeDtypeStruct((t, LANES), F32), jax.ShapeDtypeStruct((t, LANES), jnp.int32)),
        grid=(t // tm,),
        in_specs=[pl.BlockSpec((tm, d), lambda i: (i, 0)), pl.BlockSpec((d, LANES), lambda i: (0, 0))],
        out_specs=(pl.BlockSpec((tm, LANES), lambda i: (i, 0)), pl.BlockSpec((tm, LANES), lambda i: (i, 0))),
        compiler_params=_params("parallel"),
        name="moe_router",
    )(h, w_pad)


def _moe_dispatch(top_e, pad):
    t = top_e.shape[0]
    flat_e = top_e.reshape(-1)
    n_slots = flat_e.shape[0]
    onehot = (flat_e[:, None] == jnp.arange(N_EXPERTS)[None, :]).astype(jnp.int32)
    rank = jnp.cumsum(onehot, axis=0) - onehot
    counts = jnp.sum(onehot, axis=0)
    padded = (counts + pad - 1) // pad * pad
    pad_end = jnp.cumsum(padded)
    pad_start = pad_end - padded
    dest = jnp.sum(onehot * (pad_start[None, :] + rank), axis=1)
    n_rows = n_slots + N_EXPERTS * pad
    row_tok = jnp.zeros((n_rows,), jnp.int32).at[dest].set(jnp.arange(n_slots, dtype=jnp.int32) // 2)
    tile_e = jnp.minimum(jnp.searchsorted(pad_end, jnp.arange(n_rows // pad) * pad, side='right'),
                         N_EXPERTS - 1).astype(jnp.int32)
    return row_tok, dest.reshape(t, 2), tile_e


def _token_mixer(h, b, s, w_in_pad, tables, diff_lambda, diff_gain, cmp_pe, cmp_w1, cmp_w2, lam_init):
    t, d = h.shape
    proj = _matmul(h, w_in_pad, tm=min(1024, t), tn=512)
    rp, kmean = _rope(proj, *tables)
    proj = proj.reshape(b, s, -1)
    rp = rp.reshape(b, s, -1)
    tk = min(ATT_TILE, s)

    def cols(x, blk, n=2):
        return x[:, :, LANES * blk:LANES * (blk + n)]

    o_a = _stick_breaking(proj, BLK_AQ, cols(proj, BLK_AK).astype(BF16), _vt_tiles(cols(proj, BLK_AV), tk))
    gain2 = jnp.tile(diff_gain, 2).reshape(1, LANES)
    o_b = _diff_attention(rp, RP_BQ, cols(rp, RP_BK).astype(BF16), _vt_tiles(cols(proj, BLK_BV), tk),
                          diff_lambda, gain2, lam_init)
    nb = s // MOBA_BLOCK
    kmean = jnp.pad(kmean.reshape(b, nb, GROUP_WIDTH), ((0, 0), (0, LANES - nb), (0, 0)))
    o_c = _moba(rp, cols(rp, RP_CK).astype(BF16), _vt_tiles(cols(proj, BLK_CV), tk), kmean)
    n_chunks = s // CMP_STRIDE
    kvc = cols(proj, BLK_KVC, 1)
    chunks = jnp.stack([kvc[:, :, :HEAD_DIM], kvc[:, :, HEAD_DIM:]]).reshape(2, b, n_chunks, CMP_STRIDE * HEAD_DIM)
    cmp = _compress(chunks, cmp_pe.reshape(2, 2, CMP_STRIDE * HEAD_DIM),
                    cmp_w1.reshape(2, 2, CMP_STRIDE * HEAD_DIM, -1), cmp_w2)
    sel_id = jnp.arange(LANES)[:, None]
    cmp_id = jnp.arange(n_chunks)[None, :]
    overlap_t = ((cmp_id * CMP_STRIDE < (sel_id + 1) * SEL_BLOCK)
                 & (cmp_id * CMP_STRIDE + CMP_LEN > sel_id * SEL_BLOCK)).astype(BF16)
    o_cmp_t, sel_t = _nsa_cmp(proj, cmp[0], cmp[1].transpose(0, 2, 1), overlap_t)
    kvs = cols(rp, RP_KVS, 1)
    kvw = cols(rp, RP_KVW, 1)

    def val_t(kv):
        v = jnp.pad(kv[:, :, HEAD_DIM:], ((0, 0), (0, 0), (0, HEAD_DIM)))
        return _vt_tiles(v, tk)[:, 0, :, :HEAD_DIM]

    o_d = _nsa_main(rp, proj, kvs.astype(BF16), val_t(kvs), kvw.astype(BF16), val_t(kvw), sel_t, o_cmp_t)
    return [o.reshape(t, GROUP_WIDTH) for o in (o_a, o_b, o_c, o_d)]


def kernel(x, p, positions, w_in, w_out, ln_mix_g, ln_mix_b, diff_lambda, diff_gain, nsa_cmp_pos, nsa_cmp_w1, nsa_cmp_w2, ffn_w1, ffn_w3, ffn_w2, moe_router, moe_w1, moe_w3, moe_w2, ln_ffn_g, ln_ffn_b, ple_proj, ple_gate):
    b, s, d = x.shape
    t = b * s
    depth = w_in.shape[0]
    alpha = (2 * depth) ** 0.25
    tables = _rope_tables(positions)
    h = x.reshape(t, d)
    ones_gate = jnp.zeros((t, LANES), F32).at[:, 0].set(1.0)
    for i in range(depth):
        w_in_pad = jnp.pad(w_in[i], ((0, 0), (0, N_IN_PAD - w_in.shape[2]))).astype(BF16)
        lam_init = 0.8 - 0.6 * math.exp(-0.3 * i)
        groups = _token_mixer(h, b, s, w_in_pad, tables, diff_lambda[i], diff_gain[i],
                              nsa_cmp_pos[i], nsa_cmp_w1[i], nsa_cmp_w2[i], lam_init)
        h = _out_proj(groups, w_out[i].astype(BF16), h, ln_mix_g[i:i + 1], ln_mix_b[i:i + 1], alpha)
        if i % 2 == 0:
            j = i // 2
            tm = min(1024, t)
            f0 = _ffn(h.astype(BF16), jnp.zeros((t // tm,), jnp.int32), ffn_w1[j:j + 1].astype(BF16),
                      ffn_w3[j:j + 1].astype(BF16), ffn_w2[j:j + 1].astype(BF16), tm)
            f1, gates = f0, ones_gate
        else:
            j = i // 2
            wr = jnp.pad(moe_router[j], ((0, 0), (0, LANES - N_EXPERTS)))
            gates, idx = _router(h, wr)
            row_tok, slot_row, tile_e = _moe_dispatch(idx[:, :2], MOE_ROWS)
            xg = h.astype(BF16)[row_tok]
            y = _ffn(xg, tile_e, moe_w1[j].astype(BF16), moe_w3[j].astype(BF16), moe_w2[j].astype(BF16), MOE_ROWS)
            f0, f1 = y[slot_row[:, 0]], y[slot_row[:, 1]]
        h = _post(h, f0, f1, gates, p[i].reshape(t, -1), ln_ffn_g[i:i + 1], ln_ffn_b[i:i + 1],
                  ple_gate[i].astype(BF16), ple_proj[i].astype(BF16), alpha)
    return h.reshape(b, s, d)
```
